```python
import jax
import jax.numpy as jnp
from jax import lax
import numpy as np

D_MODEL = 1024
BATCH = 16
SEQ = 2048
DEPTH = 4

GRID_W = 64
CTX_LEN = 256
N_MIXERS = 3
N_HG = (DEPTH + 2) // 3
N_RW = (DEPTH + 1) // 3
N_MLA = DEPTH // 3
NORM_EPS = 1e-6
D_FF = 2816
FFN_CONV_W = 3
HG_HEADS = 8
HG_DK = D_MODEL // HG_HEADS
HG_DV = D_MODEL // HG_HEADS
HG_CHUNK = 64
RW_HEAD = 64
RW_HEADS = D_MODEL // RW_HEAD
RW_DECAY_LORA = 64
RW_AAA_LORA = 64
RW_GATE_LORA = 160
RW_LN_EPS = 64e-5
MLA_HEADS = 16
MLA_NOPE = 64
MLA_ROPE = 32
MLA_V = 64
MLA_Q_LORA = 256
MLA_KV_LORA = 256
MLA_SCALE = (MLA_NOPE + MLA_ROPE) ** -0.5
ROPE_BASE = 10000.0
Q_BLOCK = 128

kernel_name = 'hybrid_hgrn2_rwkv7_mla_prefix_trunk'


def rmsnorm(x, g):
    xf = x.astype(jnp.float32)
    y = xf * lax.rsqrt(jnp.mean(xf * xf, axis=-1, keepdims=True) + NORM_EPS)
    return (y * g.astype(jnp.float32)).astype(x.dtype)


def modulate(h, shift, scale):
    return h * (1.0 + scale) + shift


def axial_rope(length, dim):
    n_rows = length // GRID_W
    row = jnp.repeat(jnp.arange(n_rows, dtype=jnp.float32), GRID_W)
    col = jnp.tile(jnp.arange(GRID_W, dtype=jnp.float32), n_rows)
    nq = dim // 4
    inv_freq = ROPE_BASE ** (-jnp.arange(nq, dtype=jnp.float32) / nq)
    ang_r = row[:, None] * inv_freq
    ang_c = col[:, None] * inv_freq
    ang = jnp.concatenate([ang_r, ang_r, ang_c, ang_c], axis=-1)
    return jnp.cos(ang), jnp.sin(ang)


def apply_rope(x, cos, sin):
    nq = x.shape[-1] // 4
    xr = x.reshape(x.shape[:-1] + (2, 2, nq))
    rot = jnp.stack([-xr[..., 1, :], xr[..., 0, :]], axis=-2).reshape(x.shape)
    return x * cos.astype(x.dtype) + rot * sin.astype(x.dtype)


def conv_ffn(h, w_in, w_conv, b_conv, w_out):
    a, v = jnp.split(h @ w_in, 2, axis=-1)
    a = lax.conv_general_dilated(a, w_conv[:, None, :], window_strides=(1,), padding=((FFN_CONV_W // 2, FFN_CONV_W // 2),), dimension_numbers=('NWC', 'WIO', 'NWC'), feature_group_count=D_FF) + b_conv
    return (jax.nn.silu(a) * v) @ w_out


def chunk_gla(q, k, v, log_f, s0):
    B, H, L, _ = q.shape
    n = L // HG_CHUNK
    mid = HG_CHUNK // 2
    causal = jnp.tril(jnp.ones((HG_CHUNK, HG_CHUNK), dtype=bool))

    def chunks(t):
        return jnp.moveaxis(t.reshape(B, H, n, HG_CHUNK, t.shape[-1]), 2, 0)

    def step(S, inp):
        qc, kc, vc, gc = inp
        b = jnp.cumsum(gc, axis=-2)
        b_mid = b[..., mid:mid + 1, :]
        b_end = b[..., -1:, :]
        att = jnp.einsum('bhtk,bhsk->bhts', qc * jnp.exp(b - b_mid), kc * jnp.exp(b_mid - b))
        att = jnp.where(causal, att, 0.0)
        o = jnp.einsum('bhts,bhsv->bhtv', att, vc) + jnp.einsum('bhtk,bhkv->bhtv', qc * jnp.exp(b), S)
        S = jnp.exp(b_end)[..., 0, :, None] * S + jnp.einsum('bhsk,bhsv->bhkv', kc * jnp.exp(b_end - b), vc)
        return S, o

    S, o = lax.scan(step, s0, (chunks(q), chunks(k), chunks(v), chunks(log_f)))
    return jnp.moveaxis(o, 0, 2).reshape(B, H, L, v.shape[-1]), S


def hgrn2_mixer(hc, hl, w_in, lb_fwd, lb_bwd, g_norm, w_o, need_ctx):
    f32 = jnp.float32

    def heads(t):
        b, l, _ = t.shape
        return t.reshape(b, l, HG_HEADS, -1).transpose(0, 2, 1, 3).astype(f32)

    def project(h):
        q, i, g, z_f, z_b = jnp.split(h @ w_in, 5, axis=-1)
        return heads(jax.nn.silu(q)), heads(i), g, heads(z_f), heads(z_b)

    def gates(z, lb):
        lb = lb.reshape(HG_HEADS, 1, HG_DK).astype(f32)
        return jnp.log(lb + (1.0 - lb) * jax.nn.sigmoid(z)), (1.0 - lb) * jax.nn.sigmoid(-z)

    def flip(t):
        return jnp.flip(t, axis=2)

    def readout(o, g):
        b, h, l, dv = o.shape
        o = rmsnorm(jnp.moveaxis(o, 1, 2), g_norm).reshape(b, l, h * dv)
        return (o.astype(g.dtype) * jax.nn.silu(g)) @ w_o

    q_c, i_c, g_c, zf_c, zb_c = project(hc)
    q_l, i_l, g_l, zf_l, zb_l = project(hl)
    s0 = jnp.zeros((hl.shape[0], HG_HEADS, HG_DK, HG_DV), f32)
    lf_c, kf_c = gates(zf_c, lb_fwd)
    lf_l, kf_l = gates(zf_l, lb_fwd)
    of_c, s_fwd = chunk_gla(q_c, kf_c, i_c, lf_c, s0)
    of_l, _ = chunk_gla(q_l, kf_l, i_l, lf_l, s_fwd)
    lbk_c, kb_c = gates(zb_c, lb_bwd)
    lbk_l, kb_l = gates(zb_l, lb_bwd)
    ob_c, s_bwd = chunk_gla(flip(q_c), flip(kb_c), flip(i_c), flip(lbk_c), s0)
    ob_l, _ = chunk_gla(flip(q_l), flip(kb_l), flip(i_l), flip(lbk_l), s_bwd)
    y_l = readout(of_l + flip(ob_l), g_l)
    y_c = readout(of_c + flip(ob_c), g_c) if need_ctx else None
    return y_c, y_l


def rwkv7_scan(r, w, k, v, kk, a, s0):
    def step(S, inp):
        rt, wt, kt, vt, kkt, at = inp
        sa = jnp.einsum('bhvk,bhk->bhv', S, -kkt)
        S = S * wt[:, :, None, :] + sa[..., None] * (kkt * at)[:, :, None, :] + vt[..., None] * kt[:, :, None, :]
        return S, jnp.einsum('bhvk,bhk->bhv', S, rt)

    xs = tuple(jnp.moveaxis(t, 1, 0) for t in (r, w, k, v, kk, a))
    S, y = lax.scan(step, s0, xs)
    return jnp.moveaxis(y, 0, 1), S


def rwkv7_mixer(hc, hl, mu, w_rkv, w0, w1, w2, a0, a1, a2, g1, g2, k_k, k_a, r_k, ln_w, ln_b, w_o, need_ctx):
    f32 = jnp.float32

    def heads(t):
        return t.reshape(t.shape[:-1] + (RW_HEADS, RW_HEAD)).astype(f32)

    k_k_h = heads(k_k)
    k_a_h = heads(k_a)

    def project(h):
        hp = jnp.pad(h, ((0, 0), (1, 1), (0, 0)))
        dx = 0.5 * (hp[:, :-2] + hp[:, 2:]) - h
        x_r, x_w, x_k, x_v, x_a, x_g = [h + dx * mu[j] for j in range(6)]
        r = heads(x_r @ w_rkv[0])
        k = heads(x_k @ w_rkv[1])
        v = heads(x_v @ w_rkv[2])
        g = jax.nn.sigmoid(x_g @ g1) @ g2
        kk = k * k_k_h
        kk = kk / jnp.maximum(jnp.sqrt(jnp.sum(kk * kk, axis=-1, keepdims=True)), 1e-12)
        per_dir = []
        for d in range(2):
            w_log = -jax.nn.softplus(-(w0[d] + jnp.tanh(x_w @ w1[d]) @ w2[d])) - 0.5
            decay = jnp.exp(-jnp.exp(heads(w_log)))
            a = heads(jax.nn.sigmoid(a0[d] + (x_a @ a1[d]) @ a2[d]))
            per_dir.append((decay, k * (1.0 + (a - 1.0) * k_a_h), a))
        return r, v, kk, g, per_dir

    def flip(t):
        return jnp.flip(t, axis=1)

    def readout(y, r, v, k_sum, g):
        b, l = y.shape[:2]
        mean = jnp.mean(y, axis=-1, keepdims=True)
        var = jnp.mean(jnp.square(y - mean), axis=-1, keepdims=True)
        yn = ((y - mean) * lax.rsqrt(var + RW_LN_EPS)).reshape(b, l, D_MODEL) * ln_w.astype(f32) + ln_b.astype(f32)
        bonus = (jnp.sum(r * k_sum * r_k.astype(f32), axis=-1, keepdims=True) * v).reshape(b, l, D_MODEL)
        return ((yn + bonus).astype(g.dtype) * g) @ w_o

    r_c, v_c, kk_c, g_c, (fw_c, bw_c) = project(hc)
    r_l, v_l, kk_l, g_l, (fw_l, bw_l) = project(hl)
    s0 = jnp.zeros((hl.shape[0], RW_HEADS, RW_HEAD, RW_HEAD), f32)
    yf_c, s_fwd = rwkv7_scan(r_c, fw_c[0], fw_c[1], v_c, kk_c, fw_c[2], s0)
    yf_l, _ = rwkv7_scan(r_l, fw_l[0], fw_l[1], v_l, kk_l, fw_l[2], s_fwd)
    yb_c, s_bwd = rwkv7_scan(flip(r_c), flip(bw_c[0]), flip(bw_c[1]), flip(v_c), flip(kk_c), flip(bw_c[2]), s0)
    yb_l, _ = rwkv7_scan(flip(r_l), flip(bw_l[0]), flip(bw_l[1]), flip(v_l), flip(kk_l), flip(bw_l[2]), s_bwd)
    y_l = readout(yf_l + flip(yb_l), r_l, v_l, fw_l[1] + bw_l[1], g_l)
    y_c = readout(yf_c + flip(yb_c), r_c, v_c, fw_c[1] + bw_c[1], g_c) if need_ctx else None
    return y_c, y_l


def softmax_attend(q, k, v):
    s = jnp.einsum('bhqd,bhkd->bhqk', q, k).astype(jnp.float32) * MLA_SCALE
    p = jax.nn.softmax(s, axis=-1)
    return jnp.einsum('bhqk,bhkd->bhqd', p.astype(v.dtype), v)


def attention_blocked(q, k, v):
    B, H, L, dq = q.shape
    nb = L // Q_BLOCK
    qb = jnp.moveaxis(q.reshape(B, H, nb, Q_BLOCK, dq), 2, 0)
    ob = lax.map(lambda qi: softmax_attend(qi, k, v), qb)
    return jnp.moveaxis(ob, 0, 2).reshape(B, H, L, v.shape[-1])


def mla_mixer(hc, hl, w_dqkv, q_norm, kv_norm, w_uq, w_ukv, w_o, need_ctx):
    def project(h, rope):
        B, L, _ = h.shape
        cq, ckv, kr = jnp.split(h @ w_dqkv, [MLA_Q_LORA, MLA_Q_LORA + MLA_KV_LORA], axis=-1)
        q = (rmsnorm(cq, q_norm) @ w_uq).reshape(B, L, MLA_HEADS, MLA_NOPE + MLA_ROPE).transpose(0, 2, 1, 3)
        kv = (rmsnorm(ckv, kv_norm) @ w_ukv).reshape(B, L, MLA_HEADS, MLA_NOPE + MLA_V).transpose(0, 2, 1, 3)
        q_nope, q_rope = jnp.split(q, [MLA_NOPE], axis=-1)
        k_nope, v = jnp.split(kv, [MLA_NOPE], axis=-1)
        k_rope = kr[:, None]
        if rope is not None:
            cos, sin = rope
            q_rope = apply_rope(q_rope, cos, sin)
            k_rope = apply_rope(k_rope, cos, sin)
        q = jnp.concatenate([q_nope, q_rope], axis=-1)
        k = jnp.concatenate([k_nope, jnp.broadcast_to(k_rope, (B, MLA_HEADS, L, MLA_ROPE))], axis=-1)
        return q, k, v

    def out(o):
        B, H, L, V = o.shape
        return jnp.moveaxis(o, 1, 2).reshape(B, L, H * V) @ w_o

    q_c, k_c, v_c = project(hc, None)
    q_l, k_l, v_l = project(hl, axial_rope(hl.shape[1], MLA_ROPE))
    y_l = out(attention_blocked(q_l, jnp.concatenate([k_c, k_l], axis=2), jnp.concatenate([v_c, v_l], axis=2)))
    y_c = out(softmax_attend(q_c, k_c, v_c)) if need_ctx else None
    return y_c, y_l


def setup_inputs(seed: int = 0) -> dict:
    key = jax.random.key(seed)
    ks = iter(jax.random.split(key, 48))
    f32 = jnp.float32
    D = D_MODEL
    HK = HG_HEADS * HG_DK

    def nrm(shape, scale):
        return scale * jax.random.normal(next(ks), shape, f32)

    def gain(shape):
        return 1.0 + nrm(shape, 0.02)

    return {
        'x': nrm((BATCH, SEQ, D), 1.0),
        'c': nrm((BATCH, D), 1.0),
        'ctx': nrm((BATCH, CTX_LEN, D), 1.0),
        'c_ctx': nrm((D,), 1.0),
        'w_mod': nrm((DEPTH, D, 6 * D), 0.5 * D ** -0.5),
        'b_mod': nrm((DEPTH, 6 * D), 0.02),
        'norm1': gain((DEPTH, D)),
        'norm2': gain((DEPTH, D)),
        'ffn_w_in': nrm((DEPTH, D, 2 * D_FF), D ** -0.5),
        'ffn_conv': nrm((DEPTH, FFN_CONV_W, D_FF), FFN_CONV_W ** -0.5),
        'ffn_conv_b': nrm((DEPTH, D_FF), 0.02),
        'ffn_w_out': nrm((DEPTH, D_FF, D), D_FF ** -0.5),
        'hg_w_in': nrm((N_HG, D, 5 * HK), D ** -0.5),
        'hg_lb': nrm((2, N_HG, HK), 1.0),
        'hg_norm': gain((N_HG, HG_DV)),
        'hg_w_o': nrm((N_HG, HK, D), HK ** -0.5),
        'rw_mu': jax.random.uniform(next(ks), (N_RW, 6, D), f32),
        'rw_w_rkv': nrm((N_RW, 3, D, D), D ** -0.5),
        'rw_w0': jax.random.uniform(next(ks), (N_RW, 2, D), f32, -6.0, -1.0),
        'rw_w1': nrm((N_RW, 2, D, RW_DECAY_LORA), D ** -0.5),
        'rw_w2': nrm((N_RW, 2, RW_DECAY_LORA, D), 0.1 * RW_DECAY_LORA ** -0.5),
        'rw_a0': nrm((N_RW, 2, D), 0.1),
        'rw_a1': nrm((N_RW, 2, D, RW_AAA_LORA), D ** -0.5),
        'rw_a2': nrm((N_RW, 2, RW_AAA_LORA, D), 0.1 * RW_AAA_LORA ** -0.5),
        'rw_g1': nrm((N_RW, D, RW_GATE_LORA), D ** -0.5),
        'rw_g2': nrm((N_RW, RW_GATE_LORA, D), RW_GATE_LORA ** -0.5),
        'rw_k_k': 0.85 + nrm((N_RW, D), 0.02),
        'rw_k_a': gain((N_RW, D)),
        'rw_r_k': nrm((N_RW, RW_HEADS, RW_HEAD), 0.1),
        'rw_ln_w': gain((N_RW, D)),
        'rw_ln_b': nrm((N_RW, D), 0.02),
        'rw_w_o': nrm((N_RW, D, D), D ** -0.5),
        'mla_w_dqkv': nrm((N_MLA, D, MLA_Q_LORA + MLA_KV_LORA + MLA_ROPE), D ** -0.5),
        'mla_q_norm': gain((N_MLA, MLA_Q_LORA)),
        'mla_kv_norm': gain((N_MLA, MLA_KV_LORA)),
        'mla_w_uq': nrm((N_MLA, MLA_Q_LORA, MLA_HEADS * (MLA_NOPE + MLA_ROPE)), MLA_Q_LORA ** -0.5),
        'mla_w_ukv': nrm((N_MLA, MLA_KV_LORA, MLA_HEADS * (MLA_NOPE + MLA_V)), MLA_KV_LORA ** -0.5),
        'mla_w_o': nrm((N_MLA, MLA_HEADS * MLA_V, D), (MLA_HEADS * MLA_V) ** -0.5),
        'norm_f': gain((D,)),
    }


def reference(x, c, ctx, c_ctx, w_mod, b_mod, norm1, norm2, ffn_w_in, ffn_conv, ffn_conv_b, ffn_w_out, hg_w_in, hg_lb, hg_norm, hg_w_o, rw_mu, rw_w_rkv, rw_w0, rw_w1, rw_w2, rw_a0, rw_a1, rw_a2, rw_g1, rw_g2, rw_k_k, rw_k_a, rw_r_k, rw_ln_w, rw_ln_b, rw_w_o, mla_w_dqkv, mla_q_norm, mla_kv_norm, mla_w_uq, mla_w_ukv, mla_w_o, norm_f):
    lb_p = jnp.cumsum(jax.nn.softmax(hg_lb.astype(jnp.float32), axis=1), axis=1)
    lower_bounds = lb_p - lb_p[:, :1]
    silu_c = jax.nn.silu(c)
    silu_cc = jax.nn.silu(c_ctx)
    x_l, x_c = x, ctx
    for layer in range(DEPTH):
        last = layer == DEPTH - 1
        kind, j = layer % N_MIXERS, layer // N_MIXERS
        mod_l = jnp.split((silu_c @ w_mod[layer] + b_mod[layer])[:, None, :], 6, axis=-1)
        mod_c = jnp.split(silu_cc @ w_mod[layer] + b_mod[layer], 6, axis=-1)
        h_l = modulate(rmsnorm(x_l, norm1[layer]), mod_l[0], mod_l[1])
        h_c = modulate(rmsnorm(x_c, norm1[layer]), mod_c[0], mod_c[1])
        if kind == 0:
            y_c, y_l = hgrn2_mixer(h_c, h_l, hg_w_in[j], lower_bounds[0, j], lower_bounds[1, j], hg_norm[j], hg_w_o[j], not last)
        elif kind == 1:
            y_c, y_l = rwkv7_mixer(h_c, h_l, rw_mu[j], rw_w_rkv[j], rw_w0[j], rw_w1[j], rw_w2[j], rw_a0[j], rw_a1[j], rw_a2[j], rw_g1[j], rw_g2[j], rw_k_k[j], rw_k_a[j], rw_r_k[j], rw_ln_w[j], rw_ln_b[j], rw_w_o[j], not last)
        else:
            y_c, y_l = mla_mixer(h_c, h_l, mla_w_dqkv[j], mla_q_norm[j], mla_kv_norm[j], mla_w_uq[j], mla_w_ukv[j], mla_w_o[j], not last)
        x_l = x_l + mod_l[2] * y_l
        x_l = x_l + mod_l[5] * conv_ffn(modulate(rmsnorm(x_l, norm2[layer]), mod_l[3], mod_l[4]), ffn_w_in[layer], ffn_conv[layer], ffn_conv_b[layer], ffn_w_out[layer])
        if not last:
            x_c = x_c + mod_c[2] * y_c
            x_c = x_c + mod_c[5] * conv_ffn(modulate(rmsnorm(x_c, norm2[layer]), mod_c[3], mod_c[4]), ffn_w_in[layer], ffn_conv[layer], ffn_conv_b[layer], ffn_w_out[layer])
    return rmsnorm(x_l, norm_f)
```

```python
import functools

import jax
import jax.numpy as jnp
from jax import lax
from jax.experimental import pallas as pl
from jax.experimental.pallas import tpu as pltpu

F32 = jnp.float32
BF16 = jnp.bfloat16

D_MODEL = 1024
N_MIXERS = 3
NORM_EPS = 1e-6
HG_HEADS = 8
HG_CHUNK = 64
RW_HEAD = 64
RW_LN_EPS = 64e-5
RW_CHUNK = 64
MLA_HEADS = 16
MLA_NOPE = 64
MLA_ROPE = 32
MLA_V = 64
MLA_Q_LORA = 256
MLA_KV_LORA = 256
MLA_SCALE = (MLA_NOPE + MLA_ROPE) ** -0.5
ROPE_BASE = 10000.0
GRID_W = 64
LANES = 128
VMEM_LIMIT = 56 * 1024 * 1024


def _params(n_axes):
    return pltpu.CompilerParams(dimension_semantics=("arbitrary",) * n_axes, vmem_limit_bytes=VMEM_LIMIT)


def _dot(a, b):
    return jnp.dot(a, b, preferred_element_type=F32)


def _dot_nt(a, b):
    return lax.dot_general(a, b, (((1,), (1,)), ((), ())), preferred_element_type=F32)


def _dot_tn(a, b):
    return lax.dot_general(a, b, (((0,), (0,)), ((), ())), preferred_element_type=F32)


def _split(x):
    hi = x.astype(BF16)
    lo = (x - hi.astype(F32)).astype(BF16)
    return hi, lo


def _dot_exact_lhs(m_bf16, x):
    hi, lo = _split(x)
    return _dot(m_bf16, hi) + _dot(m_bf16, lo)


def _dot_exact_rhs(x, m_bf16):
    hi, lo = _split(x)
    return _dot(hi, m_bf16) + _dot(lo, m_bf16)


def _silu(x):
    return x * jax.nn.sigmoid(x)


def _norm_mod(x, g, mod_ref, shift_i, scale_i):
    ms = jnp.mean(x * x, axis=-1, keepdims=True)
    y = x * lax.rsqrt(ms + NORM_EPS) * g
    return y * (1.0 + mod_ref[0, scale_i:scale_i + 1, :]) + mod_ref[0, shift_i:shift_i + 1, :]


def _mod_index(tm, L, B):
    return lambda i, *_: (jnp.minimum(i * tm // L, B), 0, 0)


def _mod_kernel(c_ref, w_ref, b_ref, o_ref):
    s = _silu(c_ref[...]).astype(BF16)
    o_ref[0] = _dot(s, w_ref[0].astype(BF16)) + b_ref[0]


def _modulation(cc, w_mod, b_mod):
    depth, d, n = w_mod.shape
    rows = cc.shape[0]
    tn = 1024
    return pl.pallas_call(
        _mod_kernel,
        grid=(depth, n // tn),
        in_specs=[
            pl.BlockSpec((rows, d), lambda l, j: (0, 0)),
            pl.BlockSpec((1, d, tn), lambda l, j: (l, 0, j)),
            pl.BlockSpec((1, 1, tn), lambda l, j: (l, 0, j)),
        ],
        out_specs=pl.BlockSpec((1, rows, tn), lambda l, j: (l, 0, j)),
        out_shape=jax.ShapeDtypeStruct((depth, rows, n), F32),
        compiler_params=_params(2),
        name="modulation",
    )(cc, w_mod, b_mod.reshape(depth, 1, n))


def _nmm_kernel(x_ref, g_ref, mod_ref, w_ref, o_ref, h_scr, *, shift_i, scale_i):
    @pl.when(pl.program_id(1) == 0)
    def _():
        h_scr[...] = _norm_mod(x_ref[...], g_ref[...], mod_ref, shift_i, scale_i).astype(BF16)

    o_ref[...] = _dot(h_scr[...], w_ref[...]).astype(o_ref.dtype)


def _norm_mod_matmul(x, g, mod, w, *, shift_i, scale_i, L, B, tm=1024, tn=512, out_dtype=F32):
    nt, d = x.shape
    n = w.shape[1]
    return pl.pallas_call(
        functools.partial(_nmm_kernel, shift_i=shift_i, scale_i=scale_i),
        grid=(nt // tm, n // tn),
        in_specs=[
            pl.BlockSpec((tm, d), lambda i, j: (i, 0)),
            pl.BlockSpec((1, d), lambda i, j: (0, 0)),
            pl.BlockSpec((1, 6, d), _mod_index(tm, L, B)),
            pl.BlockSpec((d, tn), lambda i, j: (0, j)),
        ],
        out_specs=pl.BlockSpec((tm, tn), lambda i, j: (i, j)),
        out_shape=jax.ShapeDtypeStruct((nt, n), out_dtype),
        scratch_shapes=[pltpu.VMEM((tm, d), BF16)],
        compiler_params=_params(2),
        name="norm_mod_matmul",
    )(x, g.reshape(1, d), mod, w)


def _mmres_kernel(a_ref, w_ref, x_ref, mod_ref, *rest, gate_i, final):
    acc = _dot(a_ref[...], w_ref[...])
    y = x_ref[...] + mod_ref[0, gate_i:gate_i + 1, :] * acc
    if final:
        nf_ref, o_ref = rest
        ms = jnp.mean(y * y, axis=-1, keepdims=True)
        y = y * lax.rsqrt(ms + NORM_EPS) * nf_ref[...]
    else:
        (o_ref,) = rest
    o_ref[...] = y


def _matmul_residual(a, w, x, mod, *, gate_i, L, B, n_rows, tm=512, norm_f=None):
    k = a.shape[1]
    d = w.shape[1]
    final = norm_f is not None
    in_specs = [
        pl.BlockSpec((tm, k), lambda i: (i, 0)),
        pl.BlockSpec((k, d), lambda i: (0, 0)),
        pl.BlockSpec((tm, d), lambda i: (i, 0)),
        pl.BlockSpec((1, 6, d), _mod_index(tm, L, B)),
    ]
    args = [a, w, x, mod]
    if final:
        in_specs.append(pl.BlockSpec((1, d), lambda i: (0, 0)))
        args.append(norm_f.reshape(1, d))
    return pl.pallas_call(
        functools.partial(_mmres_kernel, gate_i=gate_i, final=final),
        grid=(n_rows // tm,),
        in_specs=in_specs,
        out_specs=pl.BlockSpec((tm, d), lambda i: (i, 0)),
        out_shape=jax.ShapeDtypeStruct((n_rows, d), F32),
        compiler_params=_params(1),
        name="matmul_residual",
    )(*args)


def _ffn1_kernel(x_ref, g_ref, mod_ref, wa_ref, wv_ref, cw_ref, cb_ref, o_ref, h_scr, *, tm, n_lat_tiles, L, Lc):
    i = pl.program_id(0)

    @pl.when(pl.program_id(1) == 0)
    def _():
        h_scr[...] = _norm_mod(x_ref[...], g_ref[...], mod_ref, 3, 4).astype(BF16)

    h = h_scr[...]
    a = _dot(h, wa_ref[...])
    v = _dot(h, wv_ref[...])
    seqlen = jnp.where(i < n_lat_tiles, L, Lc)
    pos = lax.broadcasted_iota(jnp.int32, (tm, 1), 0) & (seqlen - 1)
    prev = jnp.where(pos == 0, 0.0, pltpu.roll(a, 1, 0))
    nxt = jnp.where(pos == seqlen - 1, 0.0, pltpu.roll(a, tm - 1, 0))
    ac = cw_ref[0:1, :] * prev + cw_ref[1:2, :] * a + cw_ref[2:3, :] * nxt + cb_ref[...]
    o_ref[...] = (_silu(ac) * v).astype(BF16)


def _ffn_gate(x, g, mod, w_in, conv_w, conv_b, *, L, Lc, B, n_rows, tm=2048, tf=256):
    d = x.shape[1]
    dff = conv_w.shape[1]
    nf = dff // tf
    return pl.pallas_call(
        functools.partial(_ffn1_kernel, tm=tm, n_lat_tiles=B * L // tm, L=L, Lc=Lc),
        grid=(n_rows // tm, nf),
        in_specs=[
            pl.BlockSpec((tm, d), lambda i, j: (i, 0)),
            pl.BlockSpec((1, d), lambda i, j: (0, 0)),
            pl.BlockSpec((1, 6, d), _mod_index(tm, L, B)),
            pl.BlockSpec((d, tf), lambda i, j: (0, j)),
            pl.BlockSpec((d, tf), lambda i, j: (0, j + nf)),
            pl.BlockSpec((3, tf), lambda i, j: (0, j)),
            pl.BlockSpec((1, tf), lambda i, j: (0, j)),
        ],
        out_specs=pl.BlockSpec((tm, tf), lambda i, j: (i, j)),
        out_shape=jax.ShapeDtypeStruct((n_rows, dff), BF16),
        scratch_shapes=[pltpu.VMEM((tm, d), BF16)],
        compiler_params=_params(2),
        name="ffn_gate",
    )(x, g.reshape(1, d), mod, w_in, w_in, conv_w, conv_b.reshape(1, dff))


def _tri(n, rev):
    t = lax.broadcasted_iota(jnp.int32, (n, n), 0)
    s = lax.broadcasted_iota(jnp.int32, (n, n), 1)
    return (s >= t) if rev else (s <= t)


def _gla_chunk(q, k, v, lf, st_ref, rev):
    c = q.shape[0]
    tri = _tri(c, rev)
    b = _dot_exact_lhs(tri.astype(BF16), lf)
    mid = c // 2
    mid_row = (c - 1 - mid) if rev else mid
    end_row = 0 if rev else c - 1
    b_mid = b[mid_row:mid_row + 1, :]
    b_end = b[end_row:end_row + 1, :]
    att = _dot_nt((q * jnp.exp(b - b_mid)).astype(BF16), (k * jnp.exp(b_mid - b)).astype(BF16))
    att = jnp.where(tri, att, 0.0)
    st = st_ref[...]
    o = _dot(att.astype(BF16), v.astype(BF16)) + _dot_nt((q * jnp.exp(b)).astype(BF16), st.astype(BF16))
    kd = (k * jnp.exp(b_end - b)).astype(BF16)
    st_ref[...] = st * jnp.exp(b_end) + _dot_tn(v.astype(BF16), kd)
    return o


def _hgrn_kernel(ql, il, gl, zfl, zbl, qc, ic, gc, zfc, zbc, lb_ref, gn_ref, ol_ref, oc_ref,
                 of_l, ob_l, of_c, ob_c, sf, sb):
    C = HG_CHUNK
    lbf = lb_ref[0, 0]
    lbb = lb_ref[1, 0]

    def gates(z, lb):
        return jnp.log(lb + (1.0 - lb) * jax.nn.sigmoid(z)), (1.0 - lb) * jax.nn.sigmoid(-z)

    def part(q_ref, i_ref, zf_ref, zb_ref, of_ref, ob_ref):
        n = q_ref.shape[0] // C

        def body(t, carry):
            rf = pl.ds(pl.multiple_of(t * C, C), C)
            rb = pl.ds(pl.multiple_of((n - 1 - t) * C, C), C)
            lf, kf = gates(zf_ref[rf, :], lbf)
            of_ref[rf, :] = _gla_chunk(_silu(q_ref[rf, :]), kf, i_ref[rf, :], lf, sf, False)
            lb_, kb = gates(zb_ref[rb, :], lbb)
            ob_ref[rb, :] = _gla_chunk(_silu(q_ref[rb, :]), kb, i_ref[rb, :], lb_, sb, True)
            return carry

        lax.fori_loop(0, n, body, 0)

    sf[...] = jnp.zeros_like(sf)
    sb[...] = jnp.zeros_like(sb)
    part(qc, ic, zfc, zbc, of_c, ob_c)
    part(ql, il, zfl, zbl, of_l, ob_l)

    def readout(of_ref, ob_ref, g_ref, o_ref):
        o = of_ref[...] + ob_ref[...]
        ms = jnp.mean(o * o, axis=-1, keepdims=True)
        y = o * lax.rsqrt(ms + NORM_EPS) * gn_ref[...]
        o_ref[...] = (y * _silu(g_ref[...])).astype(BF16)

    readout(of_c, ob_c, gc, oc_ref)
    readout(of_l, ob_l, gl, ol_ref)


def _hgrn_scan(u, lbs, g_norm, *, B, L, Lc):
    H = HG_HEADS
    ctx0 = B * L // Lc

    def lat(sec):
        return pl.BlockSpec((L, LANES), lambda b, h: (b, sec * H + h))

    def ctx(sec):
        return pl.BlockSpec((Lc, LANES), lambda b, h: (ctx0 + b, sec * H + h))

    secs = (0, 1, 2, 3, 4)
    return pl.pallas_call(
        _hgrn_kernel,
        grid=(B, H),
        in_specs=[lat(s) for s in secs] + [ctx(s) for s in secs] + [
            pl.BlockSpec((2, 1, 1, LANES), lambda b, h: (0, h, 0, 0)),
            pl.BlockSpec((1, LANES), lambda b, h: (0, 0)),
        ],
        out_specs=[
            pl.BlockSpec((L, LANES), lambda b, h: (b, h)),
            pl.BlockSpec((Lc, LANES), lambda b, h: (b, h)),
        ],
        out_shape=[
            jax.ShapeDtypeStruct((B * L, H * LANES), BF16),
            jax.ShapeDtypeStruct((B * Lc, H * LANES), BF16),
        ],
        scratch_shapes=[
            pltpu.VMEM((L, LANES), F32), pltpu.VMEM((L, LANES), F32),
            pltpu.VMEM((Lc, LANES), F32), pltpu.VMEM((Lc, LANES), F32),
            pltpu.VMEM((LANES, LANES), F32), pltpu.VMEM((LANES, LANES), F32),
        ],
        compiler_params=_params(2),
        name="hgrn_scan",
    )(*([u] * 10), lbs.reshape(2, H, 1, LANES), g_norm.reshape(1, LANES))


def _softplus(x):
    return jnp.maximum(x, 0.0) + jnp.log1p(jnp.exp(-jnp.abs(x)))


def _rwkv_proj_kernel(x_ref, xp_ref, xn_ref, g_ref, mod_ref, mu_ref, wrkv_ref, w0_ref, w1_ref, w2_ref,
                      a0_ref, a1_ref, a2_ref, g1_ref, g2_ref,
                      r_ref, k_ref, v_ref, gg_ref, ld_ref, a_ref, *, tm, n_lat_tiles, L, Lc):
    i = pl.program_id(0)
    g = g_ref[...]
    h = _norm_mod(x_ref[...], g, mod_ref, 0, 1)
    hp = _norm_mod(xp_ref[...], g, mod_ref, 0, 1)[7:8, :]
    hn = _norm_mod(xn_ref[...], g, mod_ref, 0, 1)[0:1, :]
    seqlen = jnp.where(i < n_lat_tiles, L, Lc)
    row = lax.broadcasted_iota(jnp.int32, (tm, 1), 0)
    pos = (i * tm + row) & (seqlen - 1)
    down = jnp.where(row == 0, hp, pltpu.roll(h, 1, 0))
    down = jnp.where(pos == 0, 0.0, down)
    up = jnp.where(row == tm - 1, hn, pltpu.roll(h, tm - 1, 0))
    up = jnp.where(pos == seqlen - 1, 0.0, up)
    dx = 0.5 * (down + up) - h

    def mix(j):
        return (h + dx * mu_ref[j:j + 1, :]).astype(BF16)

    r_ref[...] = _dot(mix(0), wrkv_ref[0])
    k_ref[...] = _dot(mix(2), wrkv_ref[1])
    v_ref[...] = _dot(mix(3), wrkv_ref[2])
    gg_ref[...] = _dot(jax.nn.sigmoid(_dot(mix(5), g1_ref[...])).astype(BF16), g2_ref[...])
    xw = mix(1)
    xa = mix(4)
    for d in range(2):
        wl = w0_ref[d:d + 1, :] + _dot(jnp.tanh(_dot(xw, w1_ref[d])).astype(BF16), w2_ref[d])
        w_log = -_softplus(-wl) - 0.5
        ld_ref[d] = -jnp.exp(w_log)
        a_ref[d] = jax.nn.sigmoid(a0_ref[d:d + 1, :] + _dot(_dot(xa, a1_ref[d]).astype(BF16), a2_ref[d]))


def _rwkv_proj(x, g, mod, p, *, L, Lc, B, tm=256):
    nt, d = x.shape
    blk8 = tm // 8

    def full(a):
        nd = a.ndim
        return pl.BlockSpec(a.shape, lambda i: (0,) * nd)

    weights = [p["mu"], p["w_rkv"], p["w0"], p["w1"], p["w2"], p["a0"], p["a1"], p["a2"], p["g1"], p["g2"]]
    row_spec = pl.BlockSpec((tm, d), lambda i: (i, 0))
    dir_spec = pl.BlockSpec((2, tm, d), lambda i: (0, i, 0))
    return pl.pallas_call(
        functools.partial(_rwkv_proj_kernel, tm=tm, n_lat_tiles=B * L // tm, L=L, Lc=Lc),
        grid=(nt // tm,),
        in_specs=[
            row_spec,
            pl.BlockSpec((8, d), lambda i: (jnp.maximum(i * blk8 - 1, 0), 0)),
            pl.BlockSpec((8, d), lambda i: (jnp.minimum((i + 1) * blk8, nt // 8 - 1), 0)),
            pl.BlockSpec((1, d), lambda i: (0, 0)),
            pl.BlockSpec((1, 6, d), _mod_index(tm, L, B)),
        ] + [full(w) for w in weights],
        out_specs=[row_spec, row_spec, row_spec, row_spec, dir_spec, dir_spec],
        out_shape=[jax.ShapeDtypeStruct((nt, d), F32)] * 4 + [jax.ShapeDtypeStruct((2, nt, d), F32)] * 2,
        compiler_params=_params(1),
        name="rwkv_proj",
    )(x, x, x, g.reshape(1, d), mod, *weights)


def _rwkv_scan_kernel(rl, kl, vl, gl, ldl, al, rc, kc, vc, gc, ldc, ac,
                      kk_ref, ka_ref, rk_ref, lnw_ref, lnb_ref, ol_ref, oc_ref,
                      G, Hs, Q, Y1, yf_l, yb_l, yf_c, yb_c, z_scr):
    C = RW_CHUNK
    C2 = 2 * C
    Lc = rc.shape[0]
    L = rl.shape[0]
    nc, nl = Lc // C, L // C
    lane = lax.broadcasted_iota(jnp.int32, (1, LANES), 1)
    m0 = lane < RW_HEAD
    ri = lax.broadcasted_iota(jnp.int32, (LANES, LANES), 0)
    ci = lax.broadcasted_iota(jnp.int32, (LANES, LANES), 1)
    same_head = (ri < RW_HEAD) == (ci < RW_HEAD)
    eye = ri == ci
    ones_bd = same_head.astype(BF16)
    k_k = kk_ref[...]
    k_a = ka_ref[...]

    def segsum(x):
        return _dot_exact_rhs(x, ones_bd)

    def x2(x):
        return jnp.concatenate([jnp.where(m0, x, 0.0), jnp.where(m0, 0.0, x)], axis=0)

    def t2(x):
        return jnp.concatenate([x, x], axis=0)

    def sel(x):
        return jnp.where(m0, x[:C, :], x[C:, :])

    def phase1(refs, d, r0, q0, slot):
        r_ref, k_ref, v_ref, ld_ref, a_ref = refs
        rev = d == 1
        rows = pl.ds(r0, C)
        qrows = pl.ds(q0, C)
        r = r_ref[rows, :]
        k = k_ref[rows, :]
        v = v_ref[rows, :]
        lw = ld_ref[d, rows, :]
        a = a_ref[d, rows, :]
        kk = k * k_k
        kk = kk / jnp.maximum(jnp.sqrt(segsum(kk * kk)), 1e-12)
        kmod = k * (1.0 + (a - 1.0) * k_a)
        beta = kk * a
        c_in = _dot_exact_lhs(_tri(C, rev).astype(BF16), lw)
        c_ex = c_in - lw
        end_row = 0 if rev else C - 1
        c_end = c_in[end_row:end_row + 1, :]
        e_neg = jnp.exp(-c_in)
        e_end = jnp.exp(c_end - c_in)
        ea = -kk * jnp.exp(c_ex)
        rb = r * jnp.exp(c_in)
        lhs = jnp.concatenate([x2(ea), x2(rb)], axis=0).astype(BF16)
        rhs = jnp.concatenate([t2(beta * e_neg), t2(kmod * e_neg)], axis=0).astype(BF16)
        M = _dot_nt(lhs, rhs)
        tt = lax.broadcasted_iota(jnp.int32, (C2, C2), 0)
        ss = lax.broadcasted_iota(jnp.int32, (C2, C2), 1)
        sh = (tt < C) == (ss < C)
        tl, sl = tt & (C - 1), ss & (C - 1)
        strict = sh & ((sl > tl) if rev else (sl < tl))
        incl = sh & ((sl >= tl) if rev else (sl <= tl))
        A = jnp.where(strict, M[:C2, :C2], 0.0)
        Bm = jnp.where(strict, M[:C2, C2:], 0.0)
        Ar = jnp.where(incl, M[C2:, :C2], 0.0)
        Br = jnp.where(incl, M[C2:, C2:], 0.0)
        T = jnp.where(tt == ss, 1.0, 0.0) + A
        P = A
        for _ in range(C.bit_length() - 2):
            Pb = P.astype(BF16)
            P = _dot(Pb, Pb)
            T = T + _dot(T.astype(BF16), P.astype(BF16))
        vb2 = t2(v).astype(BF16)
        X1 = _dot(Bm.astype(BF16), vb2)
        TU = _dot(T.astype(BF16), jnp.concatenate([X1, t2(ea)], axis=1).astype(BF16))
        U0, TA = sel(TU[:, :LANES]), sel(TU[:, LANES:])
        arbr = jnp.concatenate([Ar, Br], axis=1).astype(BF16)
        low = jnp.concatenate([vb2, jnp.zeros_like(vb2)], axis=1)
        YQ = _dot(arbr, jnp.concatenate([TU.astype(BF16), low], axis=0))
        Y1[d, qrows, :] = sel(YQ[:, :LANES])
        Q[d, qrows, :] = rb + sel(YQ[:, LANES:])
        bk = jnp.concatenate([beta * e_end, kmod * e_end], axis=0).astype(BF16)
        right = jnp.concatenate([
            jnp.concatenate([TA, U0], axis=1),
            jnp.concatenate([jnp.zeros_like(v), v], axis=1)], axis=0).astype(BF16)
        GH = _dot_tn(bk, right)
        G[d, slot] = jnp.where(eye, jnp.exp(c_end), 0.0) + jnp.where(same_head, GH[:, :LANES], 0.0)
        Hs[d, slot] = jnp.where(same_head, GH[:, LANES:], 0.0)

    lat_refs = (rl, kl, vl, ldl, al)
    ctx_refs = (rc, kc, vc, ldc, ac)

    def p1_ctx(t, carry):
        r0 = pl.multiple_of(t * C, C)
        for d in range(2):
            phase1(ctx_refs, d, r0, r0, t)
        return carry

    def p1_lat(t, carry):
        r0 = pl.multiple_of(t * C, C)
        for d in range(2):
            phase1(lat_refs, d, r0, pl.multiple_of(Lc + t * C, C), nc + t)
        return carry

    lax.fori_loop(0, nc, p1_ctx, 0)
    lax.fori_loop(0, nl, p1_lat, 0)

    def step(d, slot, q0, y_ref, y0):
        z = z_scr[d]
        qrows = pl.ds(pl.multiple_of(q0, C), C)
        y_ref[pl.ds(pl.multiple_of(y0, C), C), :] = _dot(Q[d, qrows, :].astype(BF16), z.astype(BF16)) + Y1[d, qrows, :]
        g_hi, g_lo = _split(G[d, slot])
        z_hi, z_lo = _split(z)
        z_scr[d] = _dot(g_hi, z_hi) + _dot(g_hi, z_lo) + _dot(g_lo, z_hi) + Hs[d, slot]

    z_scr[...] = jnp.zeros_like(z_scr)

    def p2_ctx(t, carry):
        step(0, t, t * C, yf_c, t * C)
        tb = nc - 1 - t
        step(1, tb, tb * C, yb_c, tb * C)
        return carry

    def p2_lat(t, carry):
        step(0, nc + t, Lc + t * C, yf_l, t * C)
        tb = nl - 1 - t
        step(1, nc + tb, Lc + tb * C, yb_l, tb * C)
        return carry

    lax.fori_loop(0, nc, p2_ctx, 0)
    lax.fori_loop(0, nl, p2_lat, 0)

    inv_n = 1.0 / RW_HEAD
    RB = 256

    def readout(refs, g_ref, yf_ref, yb_ref, o_ref):
        r_ref, k_ref, v_ref, _, a_ref = refs

        def body(t, carry):
            rows = pl.ds(pl.multiple_of(t * RB, RB), RB)
            y = yf_ref[rows, :] + yb_ref[rows, :]
            mean = segsum(y) * inv_n
            yc = y - mean
            var = segsum(yc * yc) * inv_n
            yn = yc * lax.rsqrt(var + RW_LN_EPS) * lnw_ref[...] + lnb_ref[...]
            k = k_ref[rows, :]
            k_sum = k * (1.0 + (a_ref[0, rows, :] - 1.0) * k_a) + k * (1.0 + (a_ref[1, rows, :] - 1.0) * k_a)
            bonus = segsum(r_ref[rows, :] * k_sum * rk_ref[...]) * v_ref[rows, :]
            o_ref[rows, :] = ((yn + bonus) * g_ref[rows, :]).astype(BF16)
            return carry

        lax.fori_loop(0, r_ref.shape[0] // RB, body, 0)

    readout(ctx_refs, gc, yf_c, yb_c, oc_ref)
    readout(lat_refs, gl, yf_l, yb_l, ol_ref)


def _rwkv_scan(r, k, v, g, ld, a, p, *, B, L, Lc):
    npair = D_MODEL // LANES
    ctx0 = B * L // Lc
    n_chunks = (L + Lc) // RW_CHUNK
    lat = pl.BlockSpec((L, LANES), lambda b, h: (b, h))
    ctx = pl.BlockSpec((Lc, LANES), lambda b, h: (ctx0 + b, h))
    lat2 = pl.BlockSpec((2, L, LANES), lambda b, h: (0, b, h))
    ctx2 = pl.BlockSpec((2, Lc, LANES), lambda b, h: (0, ctx0 + b, h))
    vec = pl.BlockSpec((1, LANES), lambda b, h: (0, h))
    vecs = [p["k_k"], p["k_a"], p["r_k"], p["ln_w"], p["ln_b"]]
    return pl.pallas_call(
        _rwkv_scan_kernel,
        grid=(B, npair),
        in_specs=[lat, lat, lat, lat, lat2, lat2, ctx, ctx, ctx, ctx, ctx2, ctx2] + [vec] * 5,
        out_specs=[
            pl.BlockSpec((L, LANES), lambda b, h: (b, h)),
            pl.BlockSpec((Lc, LANES), lambda b, h: (b, h)),
        ],
        out_shape=[
            jax.ShapeDtypeStruct((B * L, D_MODEL), BF16),
            jax.ShapeDtypeStruct((B * Lc, D_MODEL), BF16),
        ],
        scratch_shapes=[
            pltpu.VMEM((2, n_chunks, LANES, LANES), F32),
            pltpu.VMEM((2, n_chunks, LANES, LANES), F32),
            pltpu.VMEM((2, L + Lc, LANES), F32),
            pltpu.VMEM((2, L + Lc, LANES), F32),
            pltpu.VMEM((L, LANES), F32), pltpu.VMEM((L, LANES), F32),
            pltpu.VMEM((Lc, LANES), F32), pltpu.VMEM((Lc, LANES), F32),
            pltpu.VMEM((2, LANES, LANES), F32),
        ],
        compiler_params=_params(2),
        name="rwkv_scan",
    )(r, k, v, g, ld, a, r, k, v, g, ld, a, *[x.reshape(1, D_MODEL) for x in vecs])


def _rope_rot(x, lane):
    n = x.shape[-1]
    w = (lane - MLA_NOPE) & 15
    return jnp.where(w < 8, -pltpu.roll(x, n - 8, 1), pltpu.roll(x, 8, 1))


def _mla_proj_kernel(x_ref, g_ref, mod_ref, wd_ref, qn_ref, kvn_ref, wuq_ref, wukv_ref, cos_ref, sin_ref,
                     q_ref, kv_ref, kr_ref):
    h = _norm_mod(x_ref[...], g_ref[...], mod_ref, 0, 1).astype(BF16)
    dq = _dot(h, wd_ref[...])
    cq = dq[:, :MLA_Q_LORA]
    ckv = dq[:, MLA_Q_LORA:MLA_Q_LORA + MLA_KV_LORA]
    kr = dq[:, MLA_Q_LORA + MLA_KV_LORA:]

    def rms(x, w):
        return x * lax.rsqrt(jnp.mean(x * x, axis=-1, keepdims=True) + NORM_EPS) * w

    q = _dot(rms(cq, qn_ref[...]).astype(BF16), wuq_ref[...])
    kv_ref[...] = _dot(rms(ckv, kvn_ref[...]).astype(BF16), wukv_ref[...]).astype(BF16)
    cos = cos_ref[...]
    sin = sin_ref[...]
    lane = lax.broadcasted_iota(jnp.int32, (1, LANES), 1)
    kr_ref[...] = (kr * cos + _rope_rot(kr, lane) * sin).astype(BF16)
    lane_q = lax.broadcasted_iota(jnp.int32, (1, q.shape[1]), 1) & (LANES - 1)
    reps = q.shape[1] // LANES
    cos_q = jnp.concatenate([cos] * reps, axis=1)
    sin_q = jnp.concatenate([sin] * reps, axis=1)
    q_ref[...] = ((q * cos_q + _rope_rot(q, lane_q) * sin_q) * MLA_SCALE).astype(BF16)


def _mla_proj(x, g, mod, p, cos_t, sin_t, *, L, Lc, B, tm=512):
    nt, d = x.shape
    n_lat_tiles = B * L // tm
    per_seq = L // tm
    nq = MLA_HEADS * LANES

    def full(a):
        nd = a.ndim
        return pl.BlockSpec(a.shape, lambda i: (0,) * nd)

    def tab(i):
        return (jnp.where(i < n_lat_tiles, i % per_seq, per_seq), 0)

    weights = [p["wd"], p["q_norm"], p["kv_norm"], p["wuq"], p["wukv"]]
    return pl.pallas_call(
        _mla_proj_kernel,
        grid=(nt // tm,),
        in_specs=[
            pl.BlockSpec((tm, d), lambda i: (i, 0)),
            pl.BlockSpec((1, d), lambda i: (0, 0)),
            pl.BlockSpec((1, 6, d), _mod_index(tm, L, B)),
        ] + [full(w) for w in weights] + [pl.BlockSpec((tm, LANES), tab), pl.BlockSpec((tm, LANES), tab)],
        out_specs=[
            pl.BlockSpec((tm, nq), lambda i: (i, 0)),
            pl.BlockSpec((tm, nq), lambda i: (i, 0)),
            pl.BlockSpec((tm, LANES), lambda i: (i, 0)),
        ],
        out_shape=[
            jax.ShapeDtypeStruct((nt, nq), BF16),
            jax.ShapeDtypeStruct((nt, nq), BF16),
            jax.ShapeDtypeStruct((nt, LANES), BF16),
        ],
        compiler_params=_params(1),
        name="mla_proj",
    )(x, g.reshape(1, d), mod, *weights, cos_t, sin_t)


def _attn_kernel(*refs, has_lat):
    if has_lat:
        q_ref, kvc_ref, krc_ref, kvl_ref, krl_ref, o_ref = refs
    else:
        q_ref, kvc_ref, krc_ref, o_ref = refs
    lane = lax.broadcasted_iota(jnp.int32, (1, LANES), 1)
    m0 = lane < MLA_NOPE
    outs = []
    for hh in range(2):
        cols = slice(hh * LANES, (hh + 1) * LANES)
        q = q_ref[:, cols]
        kvc = kvc_ref[:, cols]
        s_c = _dot_nt(q, jnp.where(m0, kvc, krc_ref[...]))
        m = jnp.max(s_c, axis=-1, keepdims=True)
        if has_lat:
            kvl = kvl_ref[:, cols]
            s_l = _dot_nt(q, jnp.where(m0, kvl, krl_ref[...]))
            m = jnp.maximum(m, jnp.max(s_l, axis=-1, keepdims=True))
        p_c = jnp.exp(s_c - m)
        den = jnp.sum(p_c, axis=-1, keepdims=True)
        acc = _dot(p_c.astype(BF16), kvc)
        if has_lat:
            p_l = jnp.exp(s_l - m)
            den = den + jnp.sum(p_l, axis=-1, keepdims=True)
            acc = acc + _dot(p_l.astype(BF16), kvl)
        outs.append(acc / den)
    o_ref[...] = jnp.where(m0, pltpu.roll(outs[0], MLA_V, 1), outs[1]).astype(BF16)


def _attention(q, kv, kr, *, B, L, Lc, latent, tq=512):
    npair = MLA_HEADS // 2
    ctx0 = B * L // Lc
    w2 = 2 * LANES
    kv_ctx = pl.BlockSpec((Lc, w2), lambda b, h, t: (ctx0 + b, h))
    kr_ctx = pl.BlockSpec((Lc, LANES), lambda b, h, t: (ctx0 + b, 0))
    if latent:
        nq_t = L // tq
        in_specs = [
            pl.BlockSpec((tq, w2), lambda b, h, t: (b * nq_t + t, h)),
            kv_ctx, kr_ctx,
            pl.BlockSpec((L, w2), lambda b, h, t: (b, h)),
            pl.BlockSpec((L, LANES), lambda b, h, t: (b, 0)),
        ]
        args = (q, kv, kr, kv, kr)
        rows = B * L
        out_spec = pl.BlockSpec((tq, LANES), lambda b, h, t: (b * nq_t + t, h))
    else:
        nq_t = 1
        in_specs = [pl.BlockSpec((Lc, w2), lambda b, h, t: (ctx0 + b, h)), kv_ctx, kr_ctx]
        args = (q, kv, kr)
        rows = B * Lc
        out_spec = pl.BlockSpec((Lc, LANES), lambda b, h, t: (b, h))
    return pl.pallas_call(
        functools.partial(_attn_kernel, has_lat=latent),
        grid=(B, npair, nq_t),
        in_specs=in_specs,
        out_specs=out_spec,
        out_shape=jax.ShapeDtypeStruct((rows, npair * LANES), BF16),
        compiler_params=_params(3),
        name="mla_attention_lat" if latent else "mla_attention_ctx",
    )(*args)


def _rope_tables(L, tm):
    n_rows = L // GRID_W
    row = jnp.repeat(jnp.arange(n_rows, dtype=F32), GRID_W)
    col = jnp.tile(jnp.arange(GRID_W, dtype=F32), n_rows)
    nq = MLA_ROPE // 4
    inv_freq = ROPE_BASE ** (-jnp.arange(nq, dtype=F32) / nq)
    ang_r = row[:, None] * inv_freq
    ang_c = col[:, None] * inv_freq
    ang = jnp.concatenate([ang_r, ang_r, ang_c, ang_c], axis=-1)
    cos = jnp.ones((L + tm, LANES), F32).at[:L, MLA_NOPE:MLA_NOPE + MLA_ROPE].set(jnp.cos(ang))
    sin = jnp.zeros((L + tm, LANES), F32).at[:L, MLA_NOPE:MLA_NOPE + MLA_ROPE].set(jnp.sin(ang))
    return cos, sin


def kernel(x, c, ctx, c_ctx, w_mod, b_mod, norm1, norm2, ffn_w_in, ffn_conv, ffn_conv_b, ffn_w_out, hg_w_in, hg_lb, hg_norm, hg_w_o, rw_mu, rw_w_rkv, rw_w0, rw_w1, rw_w2, rw_a0, rw_a1, rw_a2, rw_g1, rw_g2, rw_k_k, rw_k_a, rw_r_k, rw_ln_w, rw_ln_b, rw_w_o, mla_w_dqkv, mla_q_norm, mla_kv_norm, mla_w_uq, mla_w_ukv, mla_w_o, norm_f):
    B, L, D = x.shape
    Lc = ctx.shape[1]
    depth = w_mod.shape[0]
    assert D == D_MODEL and L & (L - 1) == 0 and Lc & (Lc - 1) == 0 and L % 2048 == 0 and (B * Lc) % 2048 == 0
    n_lat = B * L
    nt = n_lat + B * Lc

    X = jnp.concatenate([x.reshape(n_lat, D), ctx.reshape(B * Lc, D)], axis=0)

    mod_rows = 8 * ((B + 1 + 7) // 8)
    cc = jnp.zeros((mod_rows, D), F32).at[:B].set(c).at[B].set(c_ctx)
    mod_all = _modulation(cc, w_mod, b_mod)[:, :B + 1].reshape(depth, B + 1, 6, D)

    lb_p = jnp.cumsum(jax.nn.softmax(hg_lb.astype(F32), axis=1), axis=1)
    lower_bounds = lb_p - lb_p[:, :1]

    for layer in range(depth):
        last = layer == depth - 1
        kind, j = layer % N_MIXERS, layer // N_MIXERS
        mod = mod_all[layer]
        n_rows = n_lat if last else nt
        kw = dict(L=L, B=B)
        if kind == 0:
            u = _norm_mod_matmul(X, norm1[layer], mod, hg_w_in[j].astype(BF16), shift_i=0, scale_i=1, **kw)
            y_l, y_c = _hgrn_scan(u, lower_bounds[:, j], hg_norm[j], B=B, L=L, Lc=Lc)
            w_o = hg_w_o[j]
        elif kind == 1:
            p = dict(mu=rw_mu[j], w_rkv=rw_w_rkv[j].astype(BF16), w0=rw_w0[j], w1=rw_w1[j].astype(BF16),
                     w2=rw_w2[j].astype(BF16), a0=rw_a0[j], a1=rw_a1[j].astype(BF16), a2=rw_a2[j].astype(BF16),
                     g1=rw_g1[j].astype(BF16), g2=rw_g2[j].astype(BF16),
                     k_k=rw_k_k[j], k_a=rw_k_a[j], r_k=rw_r_k[j].reshape(D), ln_w=rw_ln_w[j], ln_b=rw_ln_b[j])
            r, k, v, g, ld, a = _rwkv_proj(X, norm1[layer], mod, p, L=L, Lc=Lc, B=B)
            y_l, y_c = _rwkv_scan(r, k, v, g, ld, a, p, B=B, L=L, Lc=Lc)
            w_o = rw_w_o[j]
        else:
            nlq = MLA_Q_LORA + MLA_KV_LORA
            wd = jnp.zeros((D, nlq + LANES), F32).at[:, :nlq].set(mla_w_dqkv[j][:, :nlq])
            wd = wd.at[:, nlq + MLA_NOPE:nlq + MLA_NOPE + MLA_ROPE].set(mla_w_dqkv[j][:, nlq:])
            wuq = mla_w_uq[j].reshape(MLA_Q_LORA, MLA_HEADS, MLA_NOPE + MLA_ROPE)
            wuq = jnp.pad(wuq, ((0, 0), (0, 0), (0, LANES - MLA_NOPE - MLA_ROPE))).reshape(MLA_Q_LORA, MLA_HEADS * LANES)
            p = dict(wd=wd.astype(BF16), q_norm=mla_q_norm[j].reshape(1, -1), kv_norm=mla_kv_norm[j].reshape(1, -1),
                     wuq=wuq.astype(BF16), wukv=mla_w_ukv[j].astype(BF16))
            tm_p = 512
            cos_t, sin_t = _rope_tables(L, tm_p)
            q, kv, kr = _mla_proj(X, norm1[layer], mod, p, cos_t, sin_t, L=L, Lc=Lc, B=B, tm=tm_p)
            y_l = _attention(q, kv, kr, B=B, L=L, Lc=Lc, latent=True)
            y_c = _attention(q, kv, kr, B=B, L=L, Lc=Lc, latent=False)
            w_o = mla_w_o[j]
        y = y_l if last else jnp.concatenate([y_l, y_c], axis=0)
        X = _matmul_residual(y, w_o.astype(BF16), X, mod, gate_i=2, n_rows=n_rows, **kw)
        gmid = _ffn_gate(X, norm2[layer], mod, ffn_w_in[layer].astype(BF16), ffn_conv[layer], ffn_conv_b[layer],
                         L=L, Lc=Lc, B=B, n_rows=n_rows)
        X = _matmul_residual(gmid, ffn_w_out[layer].astype(BF16), X, mod, gate_i=5, n_rows=n_rows,
                             norm_f=norm_f if last else None, **kw)
    return X.reshape(B, L, D)
```

```python
import functools

import jax
import jax.numpy as jnp
from jax import lax
from jax.experimental import pallas as pl
from jax.experimental.pallas import tpu as pltpu

F32 = jnp.float32
BF16 = jnp.bfloat16

D_MODEL = 1024
N_MIXERS = 3
NORM_EPS = 1e-6
HG_HEADS = 8
HG_CHUNK = 64
RW_HEAD = 64
RW_LN_EPS = 64e-5
RW_CHUNK = 64
MLA_HEADS = 16
MLA_NOPE = 64
MLA_ROPE = 32
MLA_V = 64
MLA_Q_LORA = 256
MLA_KV_LORA = 256
MLA_SCALE = (MLA_NOPE + MLA_ROPE) ** -0.5
ROPE_BASE = 10000.0
GRID_W = 64
LANES = 128
VMEM_LIMIT = 56 * 1024 * 1024


def _params(n_axes):
    return pltpu.CompilerParams(dimension_semantics=("arbitrary",) * n_axes, vmem_limit_bytes=VMEM_LIMIT)


def _dot(a, b):
    return jnp.dot(a, b, preferred_element_type=F32)


def _dot_nt(a, b):
    return lax.dot_general(a, b, (((1,), (1,)), ((), ())), preferred_element_type=F32)


def _dot_tn(a, b):
    return lax.dot_general(a, b, (((0,), (0,)), ((), ())), preferred_element_type=F32)


def _split(x):
    hi = x.astype(BF16)
    lo = (x - hi.astype(F32)).astype(BF16)
    return hi, lo


def _dot_exact_lhs(m_bf16, x):
    hi, lo = _split(x)
    return _dot(m_bf16, hi) + _dot(m_bf16, lo)


def _dot_exact_rhs(x, m_bf16):
    hi, lo = _split(x)
    return _dot(hi, m_bf16) + _dot(lo, m_bf16)


def _silu(x):
    return x * jax.nn.sigmoid(x)


def _norm_mod(x, g, mod_ref, shift_i, scale_i):
    ms = jnp.mean(x * x, axis=-1, keepdims=True)
    y = x * lax.rsqrt(ms + NORM_EPS) * g
    return y * (1.0 + mod_ref[0, scale_i:scale_i + 1, :]) + mod_ref[0, shift_i:shift_i + 1, :]


def _mod_index(tm, L, B):
    return lambda i, *_: (jnp.minimum(i * tm // L, B), 0, 0)


def _mod_kernel(c_ref, w_ref, b_ref, o_ref):
    s = _silu(c_ref[...]).astype(BF16)
    o_ref[0] = _dot(s, w_ref[0].astype(BF16)) + b_ref[0]


def _modulation(cc, w_mod, b_mod):
    depth, d, n = w_mod.shape
    rows = cc.shape[0]
    tn = 1024
    return pl.pallas_call(
        _mod_kernel,
        grid=(depth, n // tn),
        in_specs=[
            pl.BlockSpec((rows, d), lambda l, j: (0, 0)),
            pl.BlockSpec((1, d, tn), lambda l, j: (l, 0, j)),
            pl.BlockSpec((1, 1, tn), lambda l, j: (l, 0, j)),
        ],
        out_specs=pl.BlockSpec((1, rows, tn), lambda l, j: (l, 0, j)),
        out_shape=jax.ShapeDtypeStruct((depth, rows, n), F32),
        compiler_params=_params(2),
        name="modulation",
    )(cc, w_mod, b_mod.reshape(depth, 1, n))


def _nmm_kernel(x_ref, g_ref, mod_ref, w_ref, o_ref, h_scr, *, shift_i, scale_i):
    @pl.when(pl.program_id(1) == 0)
    def _():
        h_scr[...] = _norm_mod(x_ref[...], g_ref[...], mod_ref, shift_i, scale_i).astype(BF16)

    o_ref[...] = _dot(h_scr[...], w_ref[...]).astype(o_ref.dtype)


def _norm_mod_matmul(x, g, mod, w, *, shift_i, scale_i, L, B, tm=1024, tn=512, out_dtype=F32):
    nt, d = x.shape
    n = w.shape[1]
    return pl.pallas_call(
        functools.partial(_nmm_kernel, shift_i=shift_i, scale_i=scale_i),
        grid=(nt // tm, n // tn),
        in_specs=[
            pl.BlockSpec((tm, d), lambda i, j: (i, 0)),
            pl.BlockSpec((1, d), lambda i, j: (0, 0)),
            pl.BlockSpec((1, 6, d), _mod_index(tm, L, B)),
            pl.BlockSpec((d, tn), lambda i, j: (0, j)),
        ],
        out_specs=pl.BlockSpec((tm, tn), lambda i, j: (i, j)),
        out_shape=jax.ShapeDtypeStruct((nt, n), out_dtype),
        scratch_shapes=[pltpu.VMEM((tm, d), BF16)],
        compiler_params=_params(2),
        name="norm_mod_matmul",
    )(x, g.reshape(1, d), mod, w)


def _mmres_kernel(a_ref, w_ref, x_ref, mod_ref, *rest, gate_i, final):
    acc = _dot(a_ref[...], w_ref[...])
    y = x_ref[...] + mod_ref[0, gate_i:gate_i + 1, :] * acc
    if final:
        nf_ref, o_ref = rest
        ms = jnp.mean(y * y, axis=-1, keepdims=True)
        y = y * lax.rsqrt(ms + NORM_EPS) * nf_ref[...]
    else:
        (o_ref,) = rest
    o_ref[...] = y


def _matmul_residual(a, w, x, mod, *, gate_i, L, B, n_rows, tm=512, norm_f=None):
    k = a.shape[1]
    d = w.shape[1]
    final = norm_f is not None
    in_specs = [
        pl.BlockSpec((tm, k), lambda i: (i, 0)),
        pl.BlockSpec((k, d), lambda i: (0, 0)),
        pl.BlockSpec((tm, d), lambda i: (i, 0)),
        pl.BlockSpec((1, 6, d), _mod_index(tm, L, B)),
    ]
    args = [a, w, x, mod]
    if final:
        in_specs.append(pl.BlockSpec((1, d), lambda i: (0, 0)))
        args.append(norm_f.reshape(1, d))
    return pl.pallas_call(
        functools.partial(_mmres_kernel, gate_i=gate_i, final=final),
        grid=(n_rows // tm,),
        in_specs=in_specs,
        out_specs=pl.BlockSpec((tm, d), lambda i: (i, 0)),
        out_shape=jax.ShapeDtypeStruct((n_rows, d), F32),
        compiler_params=_params(1),
        name="matmul_residual",
    )(*args)


def _ffn1_kernel(x_ref, g_ref, mod_ref, wa_ref, wv_ref, cw_ref, cb_ref, o_ref, h_scr, *, tm, n_lat_tiles, L, Lc):
    i = pl.program_id(0)

    @pl.when(pl.program_id(1) == 0)
    def _():
        h_scr[...] = _norm_mod(x_ref[...], g_ref[...], mod_ref, 3, 4).astype(BF16)

    h = h_scr[...]
    a = _dot(h, wa_ref[...])
    v = _dot(h, wv_ref[...])
    seqlen = jnp.where(i < n_lat_tiles, L, Lc)
    pos = lax.broadcasted_iota(jnp.int32, (tm, 1), 0) & (seqlen - 1)
    prev = jnp.where(pos == 0, 0.0, pltpu.roll(a, 1, 0))
    nxt = jnp.where(pos == seqlen - 1, 0.0, pltpu.roll(a, tm - 1, 0))
    ac = cw_ref[0:1, :] * prev + cw_ref[1:2, :] * a + cw_ref[2:3, :] * nxt + cb_ref[...]
    o_ref[...] = (_silu(ac) * v).astype(BF16)


def _ffn_gate(x, g, mod, w_in, conv_w, conv_b, *, L, Lc, B, n_rows, tm=2048, tf=256):
    d = x.shape[1]
    dff = conv_w.shape[1]
    nf = dff // tf
    return pl.pallas_call(
        functools.partial(_ffn1_kernel, tm=tm, n_lat_tiles=B * L // tm, L=L, Lc=Lc),
        grid=(n_rows // tm, nf),
        in_specs=[
            pl.BlockSpec((tm, d), lambda i, j: (i, 0)),
            pl.BlockSpec((1, d), lambda i, j: (0, 0)),
            pl.BlockSpec((1, 6, d), _mod_index(tm, L, B)),
            pl.BlockSpec((d, tf), lambda i, j: (0, j)),
            pl.BlockSpec((d, tf), lambda i, j: (0, j + nf)),
            pl.BlockSpec((3, tf), lambda i, j: (0, j)),
            pl.BlockSpec((1, tf), lambda i, j: (0, j)),
        ],
        out_specs=pl.BlockSpec((tm, tf), lambda i, j: (i, j)),
        out_shape=jax.ShapeDtypeStruct((n_rows, dff), BF16),
        scratch_shapes=[pltpu.VMEM((tm, d), BF16)],
        compiler_params=_params(2),
        name="ffn_gate",
    )(x, g.reshape(1, d), mod, w_in, w_in, conv_w, conv_b.reshape(1, dff))


def _tri(n, rev):
    t = lax.broadcasted_iota(jnp.int32, (n, n), 0)
    s = lax.broadcasted_iota(jnp.int32, (n, n), 1)
    return (s >= t) if rev else (s <= t)


def _gla_group(q, k, v, lf, rev):
    n = len(q)
    c = q[0].shape[0]
    mid = c // 2
    tri = {r: _tri(c, r) for r in set(rev)}
    tri_b = {r: m.astype(BF16) for r, m in tri.items()}
    mid_row = [(c - 1 - mid) if rev[i] else mid for i in range(n)]
    end_row = [0 if rev[i] else c - 1 for i in range(n)]
    b = [_dot_exact_lhs(tri_b[rev[i]], lf[i]) for i in range(n)]
    b_mid = [b[i][mid_row[i]:mid_row[i] + 1, :] for i in range(n)]
    b_end = [b[i][end_row[i]:end_row[i] + 1, :] for i in range(n)]
    qs = [(q[i] * jnp.exp(b[i] - b_mid[i])).astype(BF16) for i in range(n)]
    ks = [(k[i] * jnp.exp(b_mid[i] - b[i])).astype(BF16) for i in range(n)]
    att = [_dot_nt(qs[i], ks[i]) for i in range(n)]
    att = [jnp.where(tri[rev[i]], att[i], 0.0).astype(BF16) for i in range(n)]
    vb = [v[i].astype(BF16) for i in range(n)]
    o_intra = [_dot(att[i], vb[i]) for i in range(n)]
    kd = [(k[i] * jnp.exp(b_end[i] - b[i])).astype(BF16) for i in range(n)]
    ds = [_dot_tn(vb[i], kd[i]) for i in range(n)]
    qd = [(q[i] * jnp.exp(b[i])).astype(BF16) for i in range(n)]
    e_end = [jnp.exp(b_end[i]) for i in range(n)]
    return o_intra, qd, ds, e_end


def _hgrn_kernel(ql, il, gl, zfl, zbl, qc, ic, gc, zfc, zbc, lb_ref, gn_ref, ol_ref, oc_ref,
                 of_l, ob_l, of_c, ob_c, sf, sb):
    C = HG_CHUNK
    U = 4
    lbf = lb_ref[0, 0]
    lbb = lb_ref[1, 0]

    def gates(z, lb):
        return jnp.log(lb + (1.0 - lb) * jax.nn.sigmoid(z)), (1.0 - lb) * jax.nn.sigmoid(-z)

    def part(q_ref, i_ref, zf_ref, zb_ref, of_ref, ob_ref):
        n = q_ref.shape[0] // C

        def body(t, carry):
            rows = [pl.ds(pl.multiple_of((t * U + u) * C, C), C) for u in range(U)]
            rows += [pl.ds(pl.multiple_of((n - 1 - t * U - u) * C, C), C) for u in range(U)]
            rev = [False] * U + [True] * U
            zs = [(zb_ref if r else zf_ref)[rw, :] for rw, r in zip(rows, rev)]
            gk = [gates(z, lbb if r else lbf) for z, r in zip(zs, rev)]
            q = [_silu(q_ref[rw, :]) for rw in rows]
            v = [i_ref[rw, :] for rw in rows]
            o_intra, qd, ds, e_end = _gla_group(q, [g[1] for g in gk], v, [g[0] for g in gk], rev)
            st = [sf[...], sb[...]]
            for u in range(U):
                for d in range(2):
                    i = d * U + u
                    o = o_intra[i] + _dot_nt(qd[i], st[d].astype(BF16))
                    (ob_ref if d else of_ref)[rows[i], :] = o
                    st[d] = st[d] * e_end[i] + ds[i]
            sf[...] = st[0]
            sb[...] = st[1]
            return carry

        lax.fori_loop(0, n // U, body, 0)

    sf[...] = jnp.zeros_like(sf)
    sb[...] = jnp.zeros_like(sb)
    part(qc, ic, zfc, zbc, of_c, ob_c)
    part(ql, il, zfl, zbl, of_l, ob_l)

    def readout(of_ref, ob_ref, g_ref, o_ref):
        o = of_ref[...] + ob_ref[...]
        ms = jnp.mean(o * o, axis=-1, keepdims=True)
        y = o * lax.rsqrt(ms + NORM_EPS) * gn_ref[...]
        o_ref[...] = (y * _silu(g_ref[...])).astype(BF16)

    readout(of_c, ob_c, gc, oc_ref)
    readout(of_l, ob_l, gl, ol_ref)


def _hgrn_scan(u, lbs, g_norm, *, B, L, Lc):
    H = HG_HEADS
    ctx0 = B * L // Lc

    def lat(sec):
        return pl.BlockSpec((L, LANES), lambda b, h: (b, sec * H + h))

    def ctx(sec):
        return pl.BlockSpec((Lc, LANES), lambda b, h: (ctx0 + b, sec * H + h))

    secs = (0, 1, 2, 3, 4)
    return pl.pallas_call(
        _hgrn_kernel,
        grid=(B, H),
        in_specs=[lat(s) for s in secs] + [ctx(s) for s in secs] + [
            pl.BlockSpec((2, 1, 1, LANES), lambda b, h: (0, h, 0, 0)),
            pl.BlockSpec((1, LANES), lambda b, h: (0, 0)),
        ],
        out_specs=[
            pl.BlockSpec((L, LANES), lambda b, h: (b, h)),
            pl.BlockSpec((Lc, LANES), lambda b, h: (b, h)),
        ],
        out_shape=[
            jax.ShapeDtypeStruct((B * L, H * LANES), BF16),
            jax.ShapeDtypeStruct((B * Lc, H * LANES), BF16),
        ],
        scratch_shapes=[
            pltpu.VMEM((L, LANES), F32), pltpu.VMEM((L, LANES), F32),
            pltpu.VMEM((Lc, LANES), F32), pltpu.VMEM((Lc, LANES), F32),
            pltpu.VMEM((LANES, LANES), F32), pltpu.VMEM((LANES, LANES), F32),
        ],
        compiler_params=_params(2),
        name="hgrn_scan",
    )(*([u] * 10), lbs.reshape(2, H, 1, LANES), g_norm.reshape(1, LANES))


def _softplus(x):
    return jnp.maximum(x, 0.0) + jnp.log1p(jnp.exp(-jnp.abs(x)))


def _rwkv_proj_kernel(x_ref, xp_ref, xn_ref, g_ref, mod_ref, mu_ref, wrkv_ref, w0_ref, w1_ref, w2_ref,
                      a0_ref, a1_ref, a2_ref, g1_ref, g2_ref,
                      r_ref, k_ref, v_ref, gg_ref, ld_ref, a_ref, *, tm, n_lat_tiles, L, Lc):
    i = pl.program_id(0)
    g = g_ref[...]
    h = _norm_mod(x_ref[...], g, mod_ref, 0, 1)
    hp = _norm_mod(xp_ref[...], g, mod_ref, 0, 1)[7:8, :]
    hn = _norm_mod(xn_ref[...], g, mod_ref, 0, 1)[0:1, :]
    seqlen = jnp.where(i < n_lat_tiles, L, Lc)
    row = lax.broadcasted_iota(jnp.int32, (tm, 1), 0)
    pos = (i * tm + row) & (seqlen - 1)
    down = jnp.where(row == 0, hp, pltpu.roll(h, 1, 0))
    down = jnp.where(pos == 0, 0.0, down)
    up = jnp.where(row == tm - 1, hn, pltpu.roll(h, tm - 1, 0))
    up = jnp.where(pos == seqlen - 1, 0.0, up)
    dx = 0.5 * (down + up) - h

    def mix(j):
        return (h + dx * mu_ref[j:j + 1, :]).astype(BF16)

    r_ref[...] = _dot(mix(0), wrkv_ref[0])
    k_ref[...] = _dot(mix(2), wrkv_ref[1])
    v_ref[...] = _dot(mix(3), wrkv_ref[2])
    gg_ref[...] = _dot(jax.nn.sigmoid(_dot(mix(5), g1_ref[...])).astype(BF16), g2_ref[...])
    xw = mix(1)
    xa = mix(4)
    for d in range(2):
        wl = w0_ref[d:d + 1, :] + _dot(jnp.tanh(_dot(xw, w1_ref[d])).astype(BF16), w2_ref[d])
        w_log = -_softplus(-wl) - 0.5
        ld_ref[d] = -jnp.exp(w_log)
        a_ref[d] = jax.nn.sigmoid(a0_ref[d:d + 1, :] + _dot(_dot(xa, a1_ref[d]).astype(BF16), a2_ref[d]))


def _rwkv_proj(x, g, mod, p, *, L, Lc, B, tm=256):
    nt, d = x.shape
    blk8 = tm // 8

    def full(a):
        nd = a.ndim
        return pl.BlockSpec(a.shape, lambda i: (0,) * nd)

    weights = [p["mu"], p["w_rkv"], p["w0"], p["w1"], p["w2"], p["a0"], p["a1"], p["a2"], p["g1"], p["g2"]]
    row_spec = pl.BlockSpec((tm, d), lambda i: (i, 0))
    dir_spec = pl.BlockSpec((2, tm, d), lambda i: (0, i, 0))
    return pl.pallas_call(
        functools.partial(_rwkv_proj_kernel, tm=tm, n_lat_tiles=B * L // tm, L=L, Lc=Lc),
        grid=(nt // tm,),
        in_specs=[
            row_spec,
            pl.BlockSpec((8, d), lambda i: (jnp.maximum(i * blk8 - 1, 0), 0)),
            pl.BlockSpec((8, d), lambda i: (jnp.minimum((i + 1) * blk8, nt // 8 - 1), 0)),
            pl.BlockSpec((1, d), lambda i: (0, 0)),
            pl.BlockSpec((1, 6, d), _mod_index(tm, L, B)),
        ] + [full(w) for w in weights],
        out_specs=[row_spec, row_spec, row_spec, row_spec, dir_spec, dir_spec],
        out_shape=[jax.ShapeDtypeStruct((nt, d), F32)] * 4 + [jax.ShapeDtypeStruct((2, nt, d), F32)] * 2,
        compiler_params=_params(1),
        name="rwkv_proj",
    )(x, x, x, g.reshape(1, d), mod, *weights)


def _rwkv_scan_kernel(rl, kl, vl, gl, ldl, al, rc, kc, vc, gc, ldc, ac,
                      kk_ref, ka_ref, rk_ref, lnw_ref, lnb_ref, ol_ref, oc_ref,
                      G, Hs, Q, Y1, yf_l, yb_l, yf_c, yb_c, z_scr):
    C = RW_CHUNK
    C2 = 2 * C
    Lc = rc.shape[0]
    L = rl.shape[0]
    nc, nl = Lc // C, L // C
    lane = lax.broadcasted_iota(jnp.int32, (1, LANES), 1)
    m0 = lane < RW_HEAD
    ri = lax.broadcasted_iota(jnp.int32, (LANES, LANES), 0)
    ci = lax.broadcasted_iota(jnp.int32, (LANES, LANES), 1)
    same_head = (ri < RW_HEAD) == (ci < RW_HEAD)
    eye = ri == ci
    ones_bd = same_head.astype(BF16)
    k_k = kk_ref[...]
    k_a = ka_ref[...]

    def segsum(x):
        return _dot_exact_rhs(x, ones_bd)

    def x2(x):
        return jnp.concatenate([jnp.where(m0, x, 0.0), jnp.where(m0, 0.0, x)], axis=0)

    def t2(x):
        return jnp.concatenate([x, x], axis=0)

    def sel(x):
        return jnp.where(m0, x[:C, :], x[C:, :])

    tt = lax.broadcasted_iota(jnp.int32, (C2, C2), 0)
    ss = lax.broadcasted_iota(jnp.int32, (C2, C2), 1)
    sh = (tt < C) == (ss < C)
    tl, sl = tt & (C - 1), ss & (C - 1)
    strict = {False: sh & (sl < tl), True: sh & (sl > tl)}
    incl = {False: sh & (sl <= tl), True: sh & (sl >= tl)}
    eye2 = jnp.where(tt == ss, 1.0, 0.0)
    tri_b = {r: _tri(C, r).astype(BF16) for r in (False, True)}

    def phase1(insts):
        n = range(len(insts))
        rev = [d == 1 for (_, d, _, _, _) in insts]
        rows = [pl.ds(r0, C) for (_, _, r0, _, _) in insts]
        r = [insts[i][0][0][rows[i], :] for i in n]
        k = [insts[i][0][1][rows[i], :] for i in n]
        v = [insts[i][0][2][rows[i], :] for i in n]
        lw = [insts[i][0][3][insts[i][1], rows[i], :] for i in n]
        a = [insts[i][0][4][insts[i][1], rows[i], :] for i in n]
        kk = [k[i] * k_k for i in n]
        n2 = [segsum(kk[i] * kk[i]) for i in n]
        kk = [kk[i] / jnp.maximum(jnp.sqrt(n2[i]), 1e-12) for i in n]
        kmod = [k[i] * (1.0 + (a[i] - 1.0) * k_a) for i in n]
        beta = [kk[i] * a[i] for i in n]
        c_in = [_dot_exact_lhs(tri_b[rev[i]], lw[i]) for i in n]
        c_end = [c_in[i][(0 if rev[i] else C - 1):(0 if rev[i] else C - 1) + 1, :] for i in n]
        e_neg = [jnp.exp(-c_in[i]) for i in n]
        e_end = [jnp.exp(c_end[i] - c_in[i]) for i in n]
        ea = [-kk[i] * jnp.exp(c_in[i] - lw[i]) for i in n]
        rb = [r[i] * jnp.exp(c_in[i]) for i in n]
        lhs = [jnp.concatenate([x2(ea[i]), x2(rb[i])], axis=0).astype(BF16) for i in n]
        rhs = [jnp.concatenate([t2(beta[i] * e_neg[i]), t2(kmod[i] * e_neg[i])], axis=0).astype(BF16) for i in n]
        M = [_dot_nt(lhs[i], rhs[i]) for i in n]
        A = [jnp.where(strict[rev[i]], M[i][:C2, :C2], 0.0) for i in n]
        Bm = [jnp.where(strict[rev[i]], M[i][:C2, C2:], 0.0).astype(BF16) for i in n]
        arbr = [jnp.concatenate([jnp.where(incl[rev[i]], M[i][C2:, :C2], 0.0),
                                 jnp.where(incl[rev[i]], M[i][C2:, C2:], 0.0)], axis=1).astype(BF16) for i in n]
        T = [eye2 + A[i] for i in n]
        P = A
        for _ in range(C.bit_length() - 2):
            Pb = [P[i].astype(BF16) for i in n]
            P = [_dot(Pb[i], Pb[i]) for i in n]
            T = [T[i] + _dot(T[i].astype(BF16), P[i].astype(BF16)) for i in n]
        vb2 = [t2(v[i]).astype(BF16) for i in n]
        X1 = [_dot(Bm[i], vb2[i]) for i in n]
        TU = [_dot(T[i].astype(BF16), jnp.concatenate([X1[i], t2(ea[i])], axis=1).astype(BF16)) for i in n]
        YQ = [_dot(arbr[i], jnp.concatenate(
            [TU[i].astype(BF16), jnp.concatenate([vb2[i], jnp.zeros_like(vb2[i])], axis=1)], axis=0)) for i in n]
        bk = [jnp.concatenate([beta[i] * e_end[i], kmod[i] * e_end[i]], axis=0).astype(BF16) for i in n]
        right = [jnp.concatenate([
            jnp.concatenate([sel(TU[i][:, LANES:]), sel(TU[i][:, :LANES])], axis=1),
            jnp.concatenate([jnp.zeros_like(v[i]), v[i]], axis=1)], axis=0).astype(BF16) for i in n]
        GH = [_dot_tn(bk[i], right[i]) for i in n]
        for i in n:
            _, d, _, q0, slot = insts[i]
            qrows = pl.ds(q0, C)
            Y1[d, qrows, :] = sel(YQ[i][:, :LANES])
            Q[d, qrows, :] = rb[i] + sel(YQ[i][:, LANES:])
            G[d, slot] = jnp.where(eye, jnp.exp(c_end[i]), 0.0) + jnp.where(same_head, GH[i][:, :LANES], 0.0)
            Hs[d, slot] = jnp.where(same_head, GH[i][:, LANES:], 0.0)

    lat_refs = (rl, kl, vl, ldl, al)
    ctx_refs = (rc, kc, vc, ldc, ac)
    U = 4

    def p1_ctx(t, carry):
        items = []
        for u in range(U):
            r0 = pl.multiple_of((t * U + u) * C, C)
            items += [(ctx_refs, d, r0, r0, t * U + u) for d in range(2)]
        phase1(items)
        return carry

    def p1_lat(t, carry):
        items = []
        for u in range(U):
            r0 = pl.multiple_of((t * U + u) * C, C)
            items += [(lat_refs, d, r0, pl.multiple_of(Lc + (t * U + u) * C, C), nc + t * U + u) for d in range(2)]
        phase1(items)
        return carry

    lax.fori_loop(0, nc // U, p1_ctx, 0)
    lax.fori_loop(0, nl // U, p1_lat, 0)

    def step(d, slot, q0, y_ref, y0):
        z = z_scr[d]
        qrows = pl.ds(pl.multiple_of(q0, C), C)
        y_ref[pl.ds(pl.multiple_of(y0, C), C), :] = _dot(Q[d, qrows, :].astype(BF16), z.astype(BF16)) + Y1[d, qrows, :]
        g_hi, g_lo = _split(G[d, slot])
        z_hi, z_lo = _split(z)
        z_scr[d] = _dot(g_hi, z_hi) + _dot(g_hi, z_lo) + _dot(g_lo, z_hi) + Hs[d, slot]

    z_scr[...] = jnp.zeros_like(z_scr)

    def p2_ctx(t, carry):
        step(0, t, t * C, yf_c, t * C)
        tb = nc - 1 - t
        step(1, tb, tb * C, yb_c, tb * C)
        return carry

    def p2_lat(t, carry):
        step(0, nc + t, Lc + t * C, yf_l, t * C)
        tb = nl - 1 - t
        step(1, nc + tb, Lc + tb * C, yb_l, tb * C)
        return carry

    lax.fori_loop(0, nc, p2_ctx, 0)
    lax.fori_loop(0, nl, p2_lat, 0)

    inv_n = 1.0 / RW_HEAD
    RB = 256

    def readout(refs, g_ref, yf_ref, yb_ref, o_ref):
        r_ref, k_ref, v_ref, _, a_ref = refs

        def body(t, carry):
            rows = pl.ds(pl.multiple_of(t * RB, RB), RB)
            y = yf_ref[rows, :] + yb_ref[rows, :]
            mean = segsum(y) * inv_n
            yc = y - mean
            var = segsum(yc * yc) * inv_n
            yn = yc * lax.rsqrt(var + RW_LN_EPS) * lnw_ref[...] + lnb_ref[...]
            k = k_ref[rows, :]
            k_sum = k * (1.0 + (a_ref[0, rows, :] - 1.0) * k_a) + k * (1.0 + (a_ref[1, rows, :] - 1.0) * k_a)
            bonus = segsum(r_ref[rows, :] * k_sum * rk_ref[...]) * v_ref[rows, :]
            o_ref[rows, :] = ((yn + bonus) * g_ref[rows, :]).astype(BF16)
            return carry

        lax.fori_loop(0, r_ref.shape[0] // RB, body, 0)

    readout(ctx_refs, gc, yf_c, yb_c, oc_ref)
    readout(lat_refs, gl, yf_l, yb_l, ol_ref)


def _rwkv_scan(r, k, v, g, ld, a, p, *, B, L, Lc):
    npair = D_MODEL // LANES
    ctx0 = B * L // Lc
    n_chunks = (L + Lc) // RW_CHUNK
    lat = pl.BlockSpec((L, LANES), lambda b, h: (b, h))
    ctx = pl.BlockSpec((Lc, LANES), lambda b, h: (ctx0 + b, h))
    lat2 = pl.BlockSpec((2, L, LANES), lambda b, h: (0, b, h))
    ctx2 = pl.BlockSpec((2, Lc, LANES), lambda b, h: (0, ctx0 + b, h))
    vec = pl.BlockSpec((1, LANES), lambda b, h: (0, h))
    vecs = [p["k_k"], p["k_a"], p["r_k"], p["ln_w"], p["ln_b"]]
    return pl.pallas_call(
        _rwkv_scan_kernel,
        grid=(B, npair),
        in_specs=[lat, lat, lat, lat, lat2, lat2, ctx, ctx, ctx, ctx, ctx2, ctx2] + [vec] * 5,
        out_specs=[
            pl.BlockSpec((L, LANES), lambda b, h: (b, h)),
            pl.BlockSpec((Lc, LANES), lambda b, h: (b, h)),
        ],
        out_shape=[
            jax.ShapeDtypeStruct((B * L, D_MODEL), BF16),
            jax.ShapeDtypeStruct((B * Lc, D_MODEL), BF16),
        ],
        scratch_shapes=[
            pltpu.VMEM((2, n_chunks, LANES, LANES), F32),
            pltpu.VMEM((2, n_chunks, LANES, LANES), F32),
            pltpu.VMEM((2, L + Lc, LANES), F32),
            pltpu.VMEM((2, L + Lc, LANES), F32),
            pltpu.VMEM((L, LANES), F32), pltpu.VMEM((L, LANES), F32),
            pltpu.VMEM((Lc, LANES), F32), pltpu.VMEM((Lc, LANES), F32),
            pltpu.VMEM((2, LANES, LANES), F32),
        ],
        compiler_params=_params(2),
        name="rwkv_scan",
    )(r, k, v, g, ld, a, r, k, v, g, ld, a, *[x.reshape(1, D_MODEL) for x in vecs])


def _rope_rot(x, lane):
    n = x.shape[-1]
    w = (lane - MLA_NOPE) & 15
    return jnp.where(w < 8, -pltpu.roll(x, n - 8, 1), pltpu.roll(x, 8, 1))


def _mla_proj_kernel(x_ref, g_ref, mod_ref, wd_ref, qn_ref, kvn_ref, wuq_ref, wukv_ref, cos_ref, sin_ref,
                     q_ref, kv_ref, kr_ref):
    h = _norm_mod(x_ref[...], g_ref[...], mod_ref, 0, 1).astype(BF16)
    dq = _dot(h, wd_ref[...])
    cq = dq[:, :MLA_Q_LORA]
    ckv = dq[:, MLA_Q_LORA:MLA_Q_LORA + MLA_KV_LORA]
    kr = dq[:, MLA_Q_LORA + MLA_KV_LORA:]

    def rms(x, w):
        return x * lax.rsqrt(jnp.mean(x * x, axis=-1, keepdims=True) + NORM_EPS) * w

    q = _dot(rms(cq, qn_ref[...]).astype(BF16), wuq_ref[...])
    kv_ref[...] = _dot(rms(ckv, kvn_ref[...]).astype(BF16), wukv_ref[...]).astype(BF16)
    cos = cos_ref[...]
    sin = sin_ref[...]
    lane = lax.broadcasted_iota(jnp.int32, (1, LANES), 1)
    kr_ref[...] = (kr * cos + _rope_rot(kr, lane) * sin).astype(BF16)
    lane_q = lax.broadcasted_iota(jnp.int32, (1, q.shape[1]), 1) & (LANES - 1)
    reps = q.shape[1] // LANES
    cos_q = jnp.concatenate([cos] * reps, axis=1)
    sin_q = jnp.concatenate([sin] * reps, axis=1)
    q_ref[...] = ((q * cos_q + _rope_rot(q, lane_q) * sin_q) * MLA_SCALE).astype(BF16)


def _mla_proj(x, g, mod, p, cos_t, sin_t, *, L, Lc, B, tm=512):
    nt, d = x.shape
    n_lat_tiles = B * L // tm
    per_seq = L // tm
    nq = MLA_HEADS * LANES

    def full(a):
        nd = a.ndim
        return pl.BlockSpec(a.shape, lambda i: (0,) * nd)

    def tab(i):
        return (jnp.where(i < n_lat_tiles, i % per_seq, per_seq), 0)

    weights = [p["wd"], p["q_norm"], p["kv_norm"], p["wuq"], p["wukv"]]
    return pl.pallas_call(
        _mla_proj_kernel,
        grid=(nt // tm,),
        in_specs=[
            pl.BlockSpec((tm, d), lambda i: (i, 0)),
            pl.BlockSpec((1, d), lambda i: (0, 0)),
            pl.BlockSpec((1, 6, d), _mod_index(tm, L, B)),
        ] + [full(w) for w in weights] + [pl.BlockSpec((tm, LANES), tab), pl.BlockSpec((tm, LANES), tab)],
        out_specs=[
            pl.BlockSpec((tm, nq), lambda i: (i, 0)),
            pl.BlockSpec((tm, nq), lambda i: (i, 0)),
            pl.BlockSpec((tm, LANES), lambda i: (i, 0)),
        ],
        out_shape=[
            jax.ShapeDtypeStruct((nt, nq), BF16),
            jax.ShapeDtypeStruct((nt, nq), BF16),
            jax.ShapeDtypeStruct((nt, LANES), BF16),
        ],
        compiler_params=_params(1),
        name="mla_proj",
    )(x, g.reshape(1, d), mod, *weights, cos_t, sin_t)


def _attn_kernel(*refs, has_lat):
    if has_lat:
        q_ref, kvc_ref, krc_ref, kvl_ref, krl_ref, o_ref = refs
    else:
        q_ref, kvc_ref, krc_ref, o_ref = refs
    lane = lax.broadcasted_iota(jnp.int32, (1, LANES), 1)
    m0 = lane < MLA_NOPE
    outs = []
    for hh in range(2):
        cols = slice(hh * LANES, (hh + 1) * LANES)
        q = q_ref[:, cols]
        kvc = kvc_ref[:, cols]
        s_c = _dot_nt(q, jnp.where(m0, kvc, krc_ref[...]))
        m = jnp.max(s_c, axis=-1, keepdims=True)
        if has_lat:
            kvl = kvl_ref[:, cols]
            s_l = _dot_nt(q, jnp.where(m0, kvl, krl_ref[...]))
            m = jnp.maximum(m, jnp.max(s_l, axis=-1, keepdims=True))
        p_c = jnp.exp(s_c - m)
        den = jnp.sum(p_c, axis=-1, keepdims=True)
        acc = _dot(p_c.astype(BF16), kvc)
        if has_lat:
            p_l = jnp.exp(s_l - m)
            den = den + jnp.sum(p_l, axis=-1, keepdims=True)
            acc = acc + _dot(p_l.astype(BF16), kvl)
        outs.append(acc / den)
    o_ref[...] = jnp.where(m0, pltpu.roll(outs[0], MLA_V, 1), outs[1]).astype(BF16)


def _attention(q, kv, kr, *, B, L, Lc, latent, tq=512):
    npair = MLA_HEADS // 2
    ctx0 = B * L // Lc
    w2 = 2 * LANES
    kv_ctx = pl.BlockSpec((Lc, w2), lambda b, h, t: (ctx0 + b, h))
    kr_ctx = pl.BlockSpec((Lc, LANES), lambda b, h, t: (ctx0 + b, 0))
    if latent:
        nq_t = L // tq
        in_specs = [
            pl.BlockSpec((tq, w2), lambda b, h, t: (b * nq_t + t, h)),
            kv_ctx, kr_ctx,
            pl.BlockSpec((L, w2), lambda b, h, t: (b, h)),
            pl.BlockSpec((L, LANES), lambda b, h, t: (b, 0)),
        ]
        args = (q, kv, kr, kv, kr)
        rows = B * L
        out_spec = pl.BlockSpec((tq, LANES), lambda b, h, t: (b * nq_t + t, h))
    else:
        nq_t = 1
        in_specs = [pl.BlockSpec((Lc, w2), lambda b, h, t: (ctx0 + b, h)), kv_ctx, kr_ctx]
        args = (q, kv, kr)
        rows = B * Lc
        out_spec = pl.BlockSpec((Lc, LANES), lambda b, h, t: (b, h))
    return pl.pallas_call(
        functools.partial(_attn_kernel, has_lat=latent),
        grid=(B, npair, nq_t),
        in_specs=in_specs,
        out_specs=out_spec,
        out_shape=jax.ShapeDtypeStruct((rows, npair * LANES), BF16),
        compiler_params=_params(3),
        name="mla_attention_lat" if latent else "mla_attention_ctx",
    )(*args)


def _rope_tables(L, tm):
    n_rows = L // GRID_W
    row = jnp.repeat(jnp.arange(n_rows, dtype=F32), GRID_W)
    col = jnp.tile(jnp.arange(GRID_W, dtype=F32), n_rows)
    nq = MLA_ROPE // 4
    inv_freq = ROPE_BASE ** (-jnp.arange(nq, dtype=F32) / nq)
    ang_r = row[:, None] * inv_freq
    ang_c = col[:, None] * inv_freq
    ang = jnp.concatenate([ang_r, ang_r, ang_c, ang_c], axis=-1)
    cos = jnp.ones((L + tm, LANES), F32).at[:L, MLA_NOPE:MLA_NOPE + MLA_ROPE].set(jnp.cos(ang))
    sin = jnp.zeros((L + tm, LANES), F32).at[:L, MLA_NOPE:MLA_NOPE + MLA_ROPE].set(jnp.sin(ang))
    return cos, sin


def kernel(x, c, ctx, c_ctx, w_mod, b_mod, norm1, norm2, ffn_w_in, ffn_conv, ffn_conv_b, ffn_w_out, hg_w_in, hg_lb, hg_norm, hg_w_o, rw_mu, rw_w_rkv, rw_w0, rw_w1, rw_w2, rw_a0, rw_a1, rw_a2, rw_g1, rw_g2, rw_k_k, rw_k_a, rw_r_k, rw_ln_w, rw_ln_b, rw_w_o, mla_w_dqkv, mla_q_norm, mla_kv_norm, mla_w_uq, mla_w_ukv, mla_w_o, norm_f):
    B, L, D = x.shape
    Lc = ctx.shape[1]
    depth = w_mod.shape[0]
    assert D == D_MODEL and L & (L - 1) == 0 and Lc & (Lc - 1) == 0 and L % 2048 == 0 and (B * Lc) % 2048 == 0
    n_lat = B * L
    nt = n_lat + B * Lc

    X = jnp.concatenate([x.reshape(n_lat, D), ctx.reshape(B * Lc, D)], axis=0)

    mod_rows = 8 * ((B + 1 + 7) // 8)
    cc = jnp.zeros((mod_rows, D), F32).at[:B].set(c).at[B].set(c_ctx)
    mod_all = _modulation(cc, w_mod, b_mod)[:, :B + 1].reshape(depth, B + 1, 6, D)

    lb_p = jnp.cumsum(jax.nn.softmax(hg_lb.astype(F32), axis=1), axis=1)
    lower_bounds = lb_p - lb_p[:, :1]

    for layer in range(depth):
        last = layer == depth - 1
        kind, j = layer % N_MIXERS, layer // N_MIXERS
        mod = mod_all[layer]
        n_rows = n_lat if last else nt
        kw = dict(L=L, B=B)
        if kind == 0:
            u = _norm_mod_matmul(X, norm1[layer], mod, hg_w_in[j].astype(BF16), shift_i=0, scale_i=1, **kw)
            y_l, y_c = _hgrn_scan(u, lower_bounds[:, j], hg_norm[j], B=B, L=L, Lc=Lc)
            w_o = hg_w_o[j]
        elif kind == 1:
            p = dict(mu=rw_mu[j], w_rkv=rw_w_rkv[j].astype(BF16), w0=rw_w0[j], w1=rw_w1[j].astype(BF16),
                     w2=rw_w2[j].astype(BF16), a0=rw_a0[j], a1=rw_a1[j].astype(BF16), a2=rw_a2[j].astype(BF16),
                     g1=rw_g1[j].astype(BF16), g2=rw_g2[j].astype(BF16),
                     k_k=rw_k_k[j], k_a=rw_k_a[j], r_k=rw_r_k[j].reshape(D), ln_w=rw_ln_w[j], ln_b=rw_ln_b[j])
            r, k, v, g, ld, a = _rwkv_proj(X, norm1[layer], mod, p, L=L, Lc=Lc, B=B)
            y_l, y_c = _rwkv_scan(r, k, v, g, ld, a, p, B=B, L=L, Lc=Lc)
            w_o = rw_w_o[j]
        else:
            nlq = MLA_Q_LORA + MLA_KV_LORA
            wd = jnp.zeros((D, nlq + LANES), F32).at[:, :nlq].set(mla_w_dqkv[j][:, :nlq])
            wd = wd.at[:, nlq + MLA_NOPE:nlq + MLA_NOPE + MLA_ROPE].set(mla_w_dqkv[j][:, nlq:])
            wuq = mla_w_uq[j].reshape(MLA_Q_LORA, MLA_HEADS, MLA_NOPE + MLA_ROPE)
            wuq = jnp.pad(wuq, ((0, 0), (0, 0), (0, LANES - MLA_NOPE - MLA_ROPE))).reshape(MLA_Q_LORA, MLA_HEADS * LANES)
            p = dict(wd=wd.astype(BF16), q_norm=mla_q_norm[j].reshape(1, -1), kv_norm=mla_kv_norm[j].reshape(1, -1),
                     wuq=wuq.astype(BF16), wukv=mla_w_ukv[j].astype(BF16))
            tm_p = 512
            cos_t, sin_t = _rope_tables(L, tm_p)
            q, kv, kr = _mla_proj(X, norm1[layer], mod, p, cos_t, sin_t, L=L, Lc=Lc, B=B, tm=tm_p)
            y_l = _attention(q, kv, kr, B=B, L=L, Lc=Lc, latent=True)
            y_c = _attention(q, kv, kr, B=B, L=L, Lc=Lc, latent=False)
            w_o = mla_w_o[j]
        y = y_l if last else jnp.concatenate([y_l, y_c], axis=0)
        X = _matmul_residual(y, w_o.astype(BF16), X, mod, gate_i=2, n_rows=n_rows, **kw)
        gmid = _ffn_gate(X, norm2[layer], mod, ffn_w_in[layer].astype(BF16), ffn_conv[layer], ffn_conv_b[layer],
                         L=L, Lc=Lc, B=B, n_rows=n_rows)
        X = _matmul_residual(gmid, ffn_w_out[layer].astype(BF16), X, mod, gate_i=5, n_rows=n_rows,
                             norm_f=norm_f if last else None, **kw)
    return X.reshape(B, L, D)
```

```python
import functools

import jax
import jax.numpy as jnp
from jax import lax
from jax.experimental import pallas as pl
from jax.experimental.pallas import tpu as pltpu

F32 = jnp.float32
BF16 = jnp.bfloat16

D_MODEL = 1024
N_MIXERS = 3
NORM_EPS = 1e-6
HG_HEADS = 8
HG_CHUNK = 64
RW_HEAD = 64
RW_LN_EPS = 64e-5
RW_CHUNK = 64
MLA_HEADS = 16
MLA_NOPE = 64
MLA_ROPE = 32
MLA_V = 64
MLA_Q_LORA = 256
MLA_KV_LORA = 256
MLA_SCALE = (MLA_NOPE + MLA_ROPE) ** -0.5
ROPE_BASE = 10000.0
LOG2E = 1.4426950408889634
GRID_W = 64
LANES = 128
VMEM_LIMIT = 56 * 1024 * 1024


def _params(n_axes):
    return pltpu.CompilerParams(dimension_semantics=("arbitrary",) * n_axes, vmem_limit_bytes=VMEM_LIMIT)


def _dot(a, b):
    return jnp.dot(a, b, preferred_element_type=F32)


def _dot_nt(a, b):
    return lax.dot_general(a, b, (((1,), (1,)), ((), ())), preferred_element_type=F32)


def _dot_tn(a, b):
    return lax.dot_general(a, b, (((0,), (0,)), ((), ())), preferred_element_type=F32)


def _split(x):
    hi = x.astype(BF16)
    lo = (x - hi.astype(F32)).astype(BF16)
    return hi, lo


def _dot_exact_lhs(m_bf16, x):
    hi, lo = _split(x)
    return _dot(m_bf16, hi) + _dot(m_bf16, lo)


def _dot_exact_rhs(x, m_bf16):
    hi, lo = _split(x)
    return _dot(hi, m_bf16) + _dot(lo, m_bf16)


def _silu(x):
    return x * jax.nn.sigmoid(x)


def _norm_mod(x, g, mod_ref, shift_i, scale_i):
    ms = jnp.mean(x * x, axis=-1, keepdims=True)
    y = x * lax.rsqrt(ms + NORM_EPS) * g
    return y * (1.0 + mod_ref[0, scale_i:scale_i + 1, :]) + mod_ref[0, shift_i:shift_i + 1, :]


def _mod_index(tm, L, B):
    return lambda i, *_: (jnp.minimum(i * tm // L, B), 0, 0)


def _mod_kernel(c_ref, w_ref, b_ref, o_ref):
    s = _silu(c_ref[...]).astype(BF16)
    o_ref[0] = _dot(s, w_ref[0].astype(BF16)) + b_ref[0]


def _modulation(cc, w_mod, b_mod):
    depth, d, n = w_mod.shape
    rows = cc.shape[0]
    tn = 1024
    return pl.pallas_call(
        _mod_kernel,
        grid=(depth, n // tn),
        in_specs=[
            pl.BlockSpec((rows, d), lambda l, j: (0, 0)),
            pl.BlockSpec((1, d, tn), lambda l, j: (l, 0, j)),
            pl.BlockSpec((1, 1, tn), lambda l, j: (l, 0, j)),
        ],
        out_specs=pl.BlockSpec((1, rows, tn), lambda l, j: (l, 0, j)),
        out_shape=jax.ShapeDtypeStruct((depth, rows, n), F32),
        compiler_params=_params(2),
        name="modulation",
    )(cc, w_mod, b_mod.reshape(depth, 1, n))


def _nmm_kernel(x_ref, g_ref, mod_ref, w_ref, o_ref, h_scr, *, shift_i, scale_i):
    @pl.when(pl.program_id(1) == 0)
    def _():
        h_scr[...] = _norm_mod(x_ref[...], g_ref[...], mod_ref, shift_i, scale_i).astype(BF16)

    o_ref[...] = _dot(h_scr[...], w_ref[...]).astype(o_ref.dtype)


def _norm_mod_matmul(x, g, mod, w, *, shift_i, scale_i, L, B, tm=1024, tn=512, out_dtype=F32):
    nt, d = x.shape
    n = w.shape[1]
    return pl.pallas_call(
        functools.partial(_nmm_kernel, shift_i=shift_i, scale_i=scale_i),
        grid=(nt // tm, n // tn),
        in_specs=[
            pl.BlockSpec((tm, d), lambda i, j: (i, 0)),
            pl.BlockSpec((1, d), lambda i, j: (0, 0)),
            pl.BlockSpec((1, 6, d), _mod_index(tm, L, B)),
            pl.BlockSpec((d, tn), lambda i, j: (0, j)),
        ],
        out_specs=pl.BlockSpec((tm, tn), lambda i, j: (i, j)),
        out_shape=jax.ShapeDtypeStruct((nt, n), out_dtype),
        scratch_shapes=[pltpu.VMEM((tm, d), BF16)],
        compiler_params=_params(2),
        name="norm_mod_matmul",
    )(x, g.reshape(1, d), mod, w)


def _mmres_kernel(*refs, gate_i, final, n_lat_tiles):
    if n_lat_tiles is None:
        a_ref, w_ref, x_ref, mod_ref, *rest = refs
        a = a_ref[...]
    else:
        al_ref, ac_ref, w_ref, x_ref, mod_ref, *rest = refs
        a = jnp.where(pl.program_id(0) < n_lat_tiles, al_ref[...], ac_ref[...])
    y = x_ref[...] + mod_ref[0, gate_i:gate_i + 1, :] * _dot(a, w_ref[...])
    if final:
        nf_ref, o_ref = rest
        ms = jnp.mean(y * y, axis=-1, keepdims=True)
        y = y * lax.rsqrt(ms + NORM_EPS) * nf_ref[...]
    else:
        (o_ref,) = rest
    o_ref[...] = y


def _matmul_residual(a, w, x, mod, *, gate_i, L, B, n_rows, tm=512, norm_f=None):
    split = isinstance(a, tuple)
    k, d = w.shape
    final = norm_f is not None
    if split:
        n_lat_tiles = a[0].shape[0] // tm
        a_specs = [
            pl.BlockSpec((tm, k), lambda i: (jnp.minimum(i, n_lat_tiles - 1), 0)),
            pl.BlockSpec((tm, k), lambda i: (jnp.maximum(i - n_lat_tiles, 0), 0)),
        ]
        a_args = list(a)
    else:
        n_lat_tiles = None
        a_specs = [pl.BlockSpec((tm, k), lambda i: (i, 0))]
        a_args = [a]
    in_specs = a_specs + [
        pl.BlockSpec((k, d), lambda i: (0, 0)),
        pl.BlockSpec((tm, d), lambda i: (i, 0)),
        pl.BlockSpec((1, 6, d), _mod_index(tm, L, B)),
    ]
    args = a_args + [w, x, mod]
    if final:
        in_specs.append(pl.BlockSpec((1, d), lambda i: (0, 0)))
        args.append(norm_f.reshape(1, d))
    return pl.pallas_call(
        functools.partial(_mmres_kernel, gate_i=gate_i, final=final, n_lat_tiles=n_lat_tiles),
        grid=(n_rows // tm,),
        in_specs=in_specs,
        out_specs=pl.BlockSpec((tm, d), lambda i: (i, 0)),
        out_shape=jax.ShapeDtypeStruct((n_rows, d), F32),
        compiler_params=_params(1),
        name="matmul_residual",
    )(*args)


def _ffn1_kernel(x_ref, g_ref, mod_ref, wa_ref, wv_ref, cw_ref, cb_ref, o_ref, h_scr, *, tm, rb, n_lat_tiles, L, Lc):
    i = pl.program_id(0)

    @pl.when(pl.program_id(1) == 0)
    def _():
        h_scr[...] = _norm_mod(x_ref[...], g_ref[...], mod_ref, 3, 4).astype(BF16)

    is_lat = i < n_lat_tiles
    sub = lax.broadcasted_iota(jnp.int32, (8, 1), 0)
    nb = tm // rb
    a_blk, v_blk = [], []
    zero_row = jnp.zeros((1, wa_ref.shape[1]), F32)

    def halo(r0, nbr_row):
        if r0 % tm == 0:
            return zero_row
        edge = jnp.where(is_lat, r0 % L == 0, r0 % Lc == 0)
        return jnp.where(edge, 0.0, nbr_row)

    def gate(r):
        a = a_blk[r]
        up = halo(r * rb, a_blk[r - 1][rb - 1:rb, :] if r > 0 else zero_row)
        dn = halo((r + 1) * rb, a_blk[r + 1][0:1, :] if r < nb - 1 else zero_row)
        prev = pltpu.roll(a, 1, 0)
        prev = jnp.concatenate([jnp.where(sub == 0, up, prev[0:8, :]), prev[8:, :]], axis=0)
        nxt = pltpu.roll(a, rb - 1, 0)
        nxt = jnp.concatenate([nxt[:rb - 8, :], jnp.where(sub == 7, dn, nxt[rb - 8:, :])], axis=0)
        ac = cw_ref[0:1, :] * prev + cw_ref[1:2, :] * a + cw_ref[2:3, :] * nxt + cb_ref[...]
        o_ref[r * rb:(r + 1) * rb, :] = (_silu(ac) * v_blk[r]).astype(BF16)

    for r in range(nb):
        h = h_scr[r * rb:(r + 1) * rb, :]
        a_blk.append(_dot(h, wa_ref[...]))
        v_blk.append(_dot(h, wv_ref[...]))
        if r > 0:
            gate(r - 1)
    gate(nb - 1)


def _ffn_gate(x, g, mod, w_in, conv_w, conv_b, *, L, Lc, B, n_rows, tm=2048, tf=256, rb=256):
    assert L % rb == 0 and Lc % rb == 0 and tm % rb == 0
    d = x.shape[1]
    dff = conv_w.shape[1]
    nf = dff // tf
    return pl.pallas_call(
        functools.partial(_ffn1_kernel, tm=tm, rb=rb, n_lat_tiles=B * L // tm, L=L, Lc=Lc),
        grid=(n_rows // tm, nf),
        in_specs=[
            pl.BlockSpec((tm, d), lambda i, j: (i, 0)),
            pl.BlockSpec((1, d), lambda i, j: (0, 0)),
            pl.BlockSpec((1, 6, d), _mod_index(tm, L, B)),
            pl.BlockSpec((d, tf), lambda i, j: (0, j)),
            pl.BlockSpec((d, tf), lambda i, j: (0, j + nf)),
            pl.BlockSpec((3, tf), lambda i, j: (0, j)),
            pl.BlockSpec((1, tf), lambda i, j: (0, j)),
        ],
        out_specs=pl.BlockSpec((tm, tf), lambda i, j: (i, j)),
        out_shape=jax.ShapeDtypeStruct((n_rows, dff), BF16),
        scratch_shapes=[pltpu.VMEM((tm, d), BF16)],
        compiler_params=_params(2),
        name="ffn_gate",
    )(x, g.reshape(1, d), mod, w_in, w_in, conv_w, conv_b.reshape(1, dff))


def _tri(n, rev):
    t = lax.broadcasted_iota(jnp.int32, (n, n), 0)
    s = lax.broadcasted_iota(jnp.int32, (n, n), 1)
    return (s >= t) if rev else (s <= t)


def _gla_group(q, k, v, lf, rev):
    n = len(q)
    c = q[0].shape[0]
    mid = c // 2
    tri = {r: _tri(c, r) for r in set(rev)}
    tri_b = {r: m.astype(BF16) for r, m in tri.items()}
    mid_row = [(c - 1 - mid) if rev[i] else mid for i in range(n)]
    end_row = [0 if rev[i] else c - 1 for i in range(n)]
    b = [_dot_exact_lhs(tri_b[rev[i]], lf[i]) for i in range(n)]
    b_mid = [b[i][mid_row[i]:mid_row[i] + 1, :] for i in range(n)]
    b_end = [b[i][end_row[i]:end_row[i] + 1, :] for i in range(n)]
    qs = [(q[i] * jnp.exp(b[i] - b_mid[i])).astype(BF16) for i in range(n)]
    ks = [(k[i] * jnp.exp(b_mid[i] - b[i])).astype(BF16) for i in range(n)]
    att = [_dot_nt(qs[i], ks[i]) for i in range(n)]
    att = [jnp.where(tri[rev[i]], att[i], 0.0).astype(BF16) for i in range(n)]
    vb = [v[i].astype(BF16) for i in range(n)]
    o_intra = [_dot(att[i], vb[i]) for i in range(n)]
    kd = [(k[i] * jnp.exp(b_end[i] - b[i])).astype(BF16) for i in range(n)]
    ds = [_dot_tn(vb[i], kd[i]) for i in range(n)]
    qd = [(q[i] * jnp.exp(b[i])).astype(BF16) for i in range(n)]
    e_end = [jnp.exp(b_end[i]) for i in range(n)]
    return o_intra, qd, ds, e_end


def _hgrn_kernel(ql, il, gl, zfl, zbl, qc, ic, gc, zfc, zbc, lb_ref, gn_ref, ol_ref, oc_ref,
                 of_l, ob_l, of_c, ob_c, sf, sb):
    C = HG_CHUNK
    lbf = lb_ref[0, 0]
    lbb = lb_ref[1, 0]

    def gates(z, lb):
        return jnp.log(lb + (1.0 - lb) * jax.nn.sigmoid(z)), (1.0 - lb) * jax.nn.sigmoid(-z)

    def part(q_ref, i_ref, zf_ref, zb_ref, of_ref, ob_ref, U):
        n = q_ref.shape[0] // C

        def body(t, carry):
            rows = [pl.ds(pl.multiple_of((t * U + u) * C, C), C) for u in range(U)]
            rows += [pl.ds(pl.multiple_of((n - 1 - t * U - u) * C, C), C) for u in range(U)]
            rev = [False] * U + [True] * U
            zs = [(zb_ref if r else zf_ref)[rw, :] for rw, r in zip(rows, rev)]
            gk = [gates(z, lbb if r else lbf) for z, r in zip(zs, rev)]
            q = [_silu(q_ref[rw, :]) for rw in rows]
            v = [i_ref[rw, :] for rw in rows]
            o_intra, qd, ds, e_end = _gla_group(q, [g[1] for g in gk], v, [g[0] for g in gk], rev)
            st = [sf[...], sb[...]]
            for u in range(U):
                for d in range(2):
                    i = d * U + u
                    o = o_intra[i] + _dot_nt(qd[i], st[d].astype(BF16))
                    (ob_ref if d else of_ref)[rows[i], :] = o
                    st[d] = st[d] * e_end[i] + ds[i]
            sf[...] = st[0]
            sb[...] = st[1]
            return carry

        lax.fori_loop(0, n // U, body, 0)

    sf[...] = jnp.zeros_like(sf)
    sb[...] = jnp.zeros_like(sb)
    part(qc, ic, zfc, zbc, of_c, ob_c, 4)
    part(ql, il, zfl, zbl, of_l, ob_l, 8)

    def readout(of_ref, ob_ref, g_ref, o_ref):
        o = of_ref[...] + ob_ref[...]
        ms = jnp.mean(o * o, axis=-1, keepdims=True)
        y = o * lax.rsqrt(ms + NORM_EPS) * gn_ref[...]
        o_ref[...] = (y * _silu(g_ref[...])).astype(BF16)

    readout(of_c, ob_c, gc, oc_ref)
    readout(of_l, ob_l, gl, ol_ref)


def _hgrn_scan(u, lbs, g_norm, *, B, L, Lc):
    H = HG_HEADS
    ctx0 = B * L // Lc

    def lat(sec):
        return pl.BlockSpec((L, LANES), lambda b, h: (b, sec * H + h))

    def ctx(sec):
        return pl.BlockSpec((Lc, LANES), lambda b, h: (ctx0 + b, sec * H + h))

    secs = (0, 1, 2, 3, 4)
    return pl.pallas_call(
        _hgrn_kernel,
        grid=(B, H),
        in_specs=[lat(s) for s in secs] + [ctx(s) for s in secs] + [
            pl.BlockSpec((2, 1, 1, LANES), lambda b, h: (0, h, 0, 0)),
            pl.BlockSpec((1, LANES), lambda b, h: (0, 0)),
        ],
        out_specs=[
            pl.BlockSpec((L, LANES), lambda b, h: (b, h)),
            pl.BlockSpec((Lc, LANES), lambda b, h: (b, h)),
        ],
        out_shape=[
            jax.ShapeDtypeStruct((B * L, H * LANES), BF16),
            jax.ShapeDtypeStruct((B * Lc, H * LANES), BF16),
        ],
        scratch_shapes=[
            pltpu.VMEM((L, LANES), F32), pltpu.VMEM((L, LANES), F32),
            pltpu.VMEM((Lc, LANES), F32), pltpu.VMEM((Lc, LANES), F32),
            pltpu.VMEM((LANES, LANES), F32), pltpu.VMEM((LANES, LANES), F32),
        ],
        compiler_params=_params(2),
        name="hgrn_scan",
    )(*([u] * 10), lbs.reshape(2, H, 1, LANES), g_norm.reshape(1, LANES))


def _softplus(x):
    return jnp.maximum(x, 0.0) + jnp.log1p(jnp.exp(-jnp.abs(x)))


def _rwkv_proj_kernel(x_ref, xp_ref, xn_ref, g_ref, mod_ref, mu_ref, wrkv_ref, w0_ref, w1_ref, w2_ref,
                      a0_ref, a1_ref, a2_ref, g1_ref, g2_ref,
                      r_ref, k_ref, v_ref, gg_ref, ld_ref, a_ref, *, tm, n_lat_tiles, L, Lc):
    i = pl.program_id(0)
    g = g_ref[...]
    h = _norm_mod(x_ref[...], g, mod_ref, 0, 1)
    hp = _norm_mod(xp_ref[...], g, mod_ref, 0, 1)[7:8, :]
    hn = _norm_mod(xn_ref[...], g, mod_ref, 0, 1)[0:1, :]
    seqlen = jnp.where(i < n_lat_tiles, L, Lc)
    row = lax.broadcasted_iota(jnp.int32, (tm, 1), 0)
    pos = (i * tm + row) & (seqlen - 1)
    down = jnp.where(row == 0, hp, pltpu.roll(h, 1, 0))
    down = jnp.where(pos == 0, 0.0, down)
    up = jnp.where(row == tm - 1, hn, pltpu.roll(h, tm - 1, 0))
    up = jnp.where(pos == seqlen - 1, 0.0, up)
    dx = 0.5 * (down + up) - h

    def mix(j):
        return (h + dx * mu_ref[j:j + 1, :]).astype(BF16)

    xw, xa, xg = mix(1), mix(4), mix(5)
    t_w = [_dot(xw, w1_ref[d]) for d in range(2)]
    t_a = [_dot(xa, a1_ref[d]) for d in range(2)]
    t_g = _dot(xg, g1_ref[...])
    wl = [_dot(jnp.tanh(t_w[d]).astype(BF16), w2_ref[d]) for d in range(2)]
    al = [_dot(t_a[d].astype(BF16), a2_ref[d]) for d in range(2)]
    gg_ref[...] = _dot(jax.nn.sigmoid(t_g).astype(BF16), g2_ref[...])
    r = _dot(mix(0), wrkv_ref[0])
    k = _dot(mix(2), wrkv_ref[1])
    v = _dot(mix(3), wrkv_ref[2])
    for d in range(2):
        w_log = -_softplus(-(w0_ref[d:d + 1, :] + wl[d])) - 0.5
        ld_ref[d] = -jnp.exp(w_log)
        a_ref[d] = jax.nn.sigmoid(a0_ref[d:d + 1, :] + al[d])
    r_ref[...] = r
    k_ref[...] = k
    v_ref[...] = v


def _rwkv_proj(x, g, mod, p, *, L, Lc, B, tm=256):
    nt, d = x.shape
    blk8 = tm // 8

    def full(a):
        nd = a.ndim
        return pl.BlockSpec(a.shape, lambda i: (0,) * nd)

    weights = [p["mu"], p["w_rkv"], p["w0"], p["w1"], p["w2"], p["a0"], p["a1"], p["a2"], p["g1"], p["g2"]]
    row_spec = pl.BlockSpec((tm, d), lambda i: (i, 0))
    dir_spec = pl.BlockSpec((2, tm, d), lambda i: (0, i, 0))
    return pl.pallas_call(
        functools.partial(_rwkv_proj_kernel, tm=tm, n_lat_tiles=B * L // tm, L=L, Lc=Lc),
        grid=(nt // tm,),
        in_specs=[
            row_spec,
            pl.BlockSpec((8, d), lambda i: (jnp.maximum(i * blk8 - 1, 0), 0)),
            pl.BlockSpec((8, d), lambda i: (jnp.minimum((i + 1) * blk8, nt // 8 - 1), 0)),
            pl.BlockSpec((1, d), lambda i: (0, 0)),
            pl.BlockSpec((1, 6, d), _mod_index(tm, L, B)),
        ] + [full(w) for w in weights],
        out_specs=[row_spec, row_spec, row_spec, row_spec, dir_spec, dir_spec],
        out_shape=[jax.ShapeDtypeStruct((nt, d), F32)] * 4 + [jax.ShapeDtypeStruct((2, nt, d), F32)] * 2,
        compiler_params=_params(1),
        name="rwkv_proj",
    )(x, x, x, g.reshape(1, d), mod, *weights)


def _rwkv_scan_kernel(rl, kl, vl, gl, ldl, al, rc, kc, vc, gc, ldc, ac,
                      kk_ref, ka_ref, rk_ref, lnw_ref, lnb_ref, ol_ref, oc_ref,
                      G, Hs, Q, Y1, yf_l, yb_l, yf_c, yb_c, z_scr):
    C = RW_CHUNK
    C2 = 2 * C
    Lc = rc.shape[0]
    L = rl.shape[0]
    nc, nl = Lc // C, L // C
    lane = lax.broadcasted_iota(jnp.int32, (1, LANES), 1)
    m0 = lane < RW_HEAD
    ri = lax.broadcasted_iota(jnp.int32, (LANES, LANES), 0)
    ci = lax.broadcasted_iota(jnp.int32, (LANES, LANES), 1)
    same_head = (ri < RW_HEAD) == (ci < RW_HEAD)
    eye = ri == ci
    ones_bd = same_head.astype(BF16)
    k_k = kk_ref[...]
    k_a = ka_ref[...]

    def segsum(x):
        return _dot_exact_rhs(x, ones_bd)

    def x2(x):
        return jnp.concatenate([jnp.where(m0, x, 0.0), jnp.where(m0, 0.0, x)], axis=0)

    def t2(x):
        return jnp.concatenate([x, x], axis=0)

    def sel(x):
        return jnp.where(m0, x[:C, :], x[C:, :])

    tt = lax.broadcasted_iota(jnp.int32, (C2, C2), 0)
    ss = lax.broadcasted_iota(jnp.int32, (C2, C2), 1)
    sh = (tt < C) == (ss < C)
    tl, sl = tt & (C - 1), ss & (C - 1)
    strict = {False: sh & (sl < tl), True: sh & (sl > tl)}
    incl = {False: sh & (sl <= tl), True: sh & (sl >= tl)}
    eye2 = jnp.where(tt == ss, 1.0, 0.0)
    tri_b = {r: _tri(C, r).astype(BF16) for r in (False, True)}

    def phase1(insts):
        n = range(len(insts))
        rev = [d == 1 for (_, d, _, _, _) in insts]
        rows = [pl.ds(r0, C) for (_, _, r0, _, _) in insts]
        r = [insts[i][0][0][rows[i], :] for i in n]
        k = [insts[i][0][1][rows[i], :] for i in n]
        v = [insts[i][0][2][rows[i], :] for i in n]
        lw = [insts[i][0][3][insts[i][1], rows[i], :] for i in n]
        a = [insts[i][0][4][insts[i][1], rows[i], :] for i in n]
        kk = [k[i] * k_k for i in n]
        n2 = [segsum(kk[i] * kk[i]) for i in n]
        kk = [kk[i] / jnp.maximum(jnp.sqrt(n2[i]), 1e-12) for i in n]
        kmod = [k[i] * (1.0 + (a[i] - 1.0) * k_a) for i in n]
        beta = [kk[i] * a[i] for i in n]
        c_in = [_dot_exact_lhs(tri_b[rev[i]], lw[i]) for i in n]
        c_end = [c_in[i][(0 if rev[i] else C - 1):(0 if rev[i] else C - 1) + 1, :] for i in n]
        e_neg = [jnp.exp(-c_in[i]) for i in n]
        e_end = [jnp.exp(c_end[i] - c_in[i]) for i in n]
        ea = [-kk[i] * jnp.exp(c_in[i] - lw[i]) for i in n]
        rb = [r[i] * jnp.exp(c_in[i]) for i in n]
        lhs = [jnp.concatenate([x2(ea[i]), x2(rb[i])], axis=0).astype(BF16) for i in n]
        rhs = [jnp.concatenate([t2(beta[i] * e_neg[i]), t2(kmod[i] * e_neg[i])], axis=0).astype(BF16) for i in n]
        M = [_dot_nt(lhs[i], rhs[i]) for i in n]
        A = [jnp.where(strict[rev[i]], M[i][:C2, :C2], 0.0) for i in n]
        Bm = [jnp.where(strict[rev[i]], M[i][:C2, C2:], 0.0).astype(BF16) for i in n]
        arbr = [jnp.concatenate([jnp.where(incl[rev[i]], M[i][C2:, :C2], 0.0),
                                 jnp.where(incl[rev[i]], M[i][C2:, C2:], 0.0)], axis=1).astype(BF16) for i in n]
        T = [eye2 + A[i] for i in n]
        P = A
        for _ in range(C.bit_length() - 2):
            Pb = [P[i].astype(BF16) for i in n]
            P = [_dot(Pb[i], Pb[i]) for i in n]
            T = [T[i] + _dot(T[i].astype(BF16), P[i].astype(BF16)) for i in n]
        vb2 = [t2(v[i]).astype(BF16) for i in n]
        X1 = [_dot(Bm[i], vb2[i]) for i in n]
        TU = [_dot(T[i].astype(BF16), jnp.concatenate([X1[i], t2(ea[i])], axis=1).astype(BF16)) for i in n]
        YQ = [_dot(arbr[i], jnp.concatenate(
            [TU[i].astype(BF16), jnp.concatenate([vb2[i], jnp.zeros_like(vb2[i])], axis=1)], axis=0)) for i in n]
        bk = [jnp.concatenate([beta[i] * e_end[i], kmod[i] * e_end[i]], axis=0).astype(BF16) for i in n]
        right = [jnp.concatenate([
            jnp.concatenate([sel(TU[i][:, LANES:]), sel(TU[i][:, :LANES])], axis=1),
            jnp.concatenate([jnp.zeros_like(v[i]), v[i]], axis=1)], axis=0).astype(BF16) for i in n]
        GH = [_dot_tn(bk[i], right[i]) for i in n]
        for i in n:
            _, d, _, q0, slot = insts[i]
            qrows = pl.ds(q0, C)
            Y1[d, qrows, :] = sel(YQ[i][:, :LANES])
            Q[d, qrows, :] = rb[i] + sel(YQ[i][:, LANES:])
            G[d, slot] = jnp.where(eye, jnp.exp(c_end[i]), 0.0) + jnp.where(same_head, GH[i][:, :LANES], 0.0)
            Hs[d, slot] = jnp.where(same_head, GH[i][:, LANES:], 0.0)

    lat_refs = (rl, kl, vl, ldl, al)
    ctx_refs = (rc, kc, vc, ldc, ac)
    U = 4

    def p1_ctx(t, carry):
        items = []
        for u in range(U):
            r0 = pl.multiple_of((t * U + u) * C, C)
            items += [(ctx_refs, d, r0, r0, t * U + u) for d in range(2)]
        phase1(items)
        return carry

    def p1_lat(t, carry):
        items = []
        for u in range(U):
            r0 = pl.multiple_of((t * U + u) * C, C)
            items += [(lat_refs, d, r0, pl.multiple_of(Lc + (t * U + u) * C, C), nc + t * U + u) for d in range(2)]
        phase1(items)
        return carry

    lax.fori_loop(0, nc // U, p1_ctx, 0)
    lax.fori_loop(0, nl // U, p1_lat, 0)

    def step(d, slot, q0, y_ref, y0):
        z = z_scr[d]
        qrows = pl.ds(pl.multiple_of(q0, C), C)
        y_ref[pl.ds(pl.multiple_of(y0, C), C), :] = _dot(Q[d, qrows, :].astype(BF16), z.astype(BF16)) + Y1[d, qrows, :]
        g_hi, g_lo = _split(G[d, slot])
        z_hi, z_lo = _split(z)
        z_scr[d] = _dot(g_hi, z_hi) + _dot(g_hi, z_lo) + _dot(g_lo, z_hi) + Hs[d, slot]

    z_scr[...] = jnp.zeros_like(z_scr)

    def p2_ctx(t, carry):
        step(0, t, t * C, yf_c, t * C)
        tb = nc - 1 - t
        step(1, tb, tb * C, yb_c, tb * C)
        return carry

    def p2_lat(t, carry):
        step(0, nc + t, Lc + t * C, yf_l, t * C)
        tb = nl - 1 - t
        step(1, nc + tb, Lc + tb * C, yb_l, tb * C)
        return carry

    lax.fori_loop(0, nc, p2_ctx, 0)
    lax.fori_loop(0, nl, p2_lat, 0)

    inv_n = 1.0 / RW_HEAD
    RB = 256

    def readout(refs, g_ref, yf_ref, yb_ref, o_ref):
        r_ref, k_ref, v_ref, _, a_ref = refs

        def body(t, carry):
            rows = pl.ds(pl.multiple_of(t * RB, RB), RB)
            y = yf_ref[rows, :] + yb_ref[rows, :]
            mean = segsum(y) * inv_n
            yc = y - mean
            var = segsum(yc * yc) * inv_n
            yn = yc * lax.rsqrt(var + RW_LN_EPS) * lnw_ref[...] + lnb_ref[...]
            k = k_ref[rows, :]
            k_sum = k * (1.0 + (a_ref[0, rows, :] - 1.0) * k_a) + k * (1.0 + (a_ref[1, rows, :] - 1.0) * k_a)
            bonus = segsum(r_ref[rows, :] * k_sum * rk_ref[...]) * v_ref[rows, :]
            o_ref[rows, :] = ((yn + bonus) * g_ref[rows, :]).astype(BF16)
            return carry

        lax.fori_loop(0, r_ref.shape[0] // RB, body, 0)

    readout(ctx_refs, gc, yf_c, yb_c, oc_ref)
    readout(lat_refs, gl, yf_l, yb_l, ol_ref)


def _rwkv_scan(r, k, v, g, ld, a, p, *, B, L, Lc):
    npair = D_MODEL // LANES
    ctx0 = B * L // Lc
    n_chunks = (L + Lc) // RW_CHUNK
    lat = pl.BlockSpec((L, LANES), lambda b, h: (b, h))
    ctx = pl.BlockSpec((Lc, LANES), lambda b, h: (ctx0 + b, h))
    lat2 = pl.BlockSpec((2, L, LANES), lambda b, h: (0, b, h))
    ctx2 = pl.BlockSpec((2, Lc, LANES), lambda b, h: (0, ctx0 + b, h))
    vec = pl.BlockSpec((1, LANES), lambda b, h: (0, h))
    vecs = [p["k_k"], p["k_a"], p["r_k"], p["ln_w"], p["ln_b"]]
    return pl.pallas_call(
        _rwkv_scan_kernel,
        grid=(B, npair),
        in_specs=[lat, lat, lat, lat, lat2, lat2, ctx, ctx, ctx, ctx, ctx2, ctx2] + [vec] * 5,
        out_specs=[
            pl.BlockSpec((L, LANES), lambda b, h: (b, h)),
            pl.BlockSpec((Lc, LANES), lambda b, h: (b, h)),
        ],
        out_shape=[
            jax.ShapeDtypeStruct((B * L, D_MODEL), BF16),
            jax.ShapeDtypeStruct((B * Lc, D_MODEL), BF16),
        ],
        scratch_shapes=[
            pltpu.VMEM((2, n_chunks, LANES, LANES), F32),
            pltpu.VMEM((2, n_chunks, LANES, LANES), F32),
            pltpu.VMEM((2, L + Lc, LANES), F32),
            pltpu.VMEM((2, L + Lc, LANES), F32),
            pltpu.VMEM((L, LANES), F32), pltpu.VMEM((L, LANES), F32),
            pltpu.VMEM((Lc, LANES), F32), pltpu.VMEM((Lc, LANES), F32),
            pltpu.VMEM((2, LANES, LANES), F32),
        ],
        compiler_params=_params(2),
        name="rwkv_scan",
    )(r, k, v, g, ld, a, r, k, v, g, ld, a, *[x.reshape(1, D_MODEL) for x in vecs])


def _rope_rot(x, lane):
    n = x.shape[-1]
    w = (lane - MLA_NOPE) & 15
    return jnp.where(w < 8, -pltpu.roll(x, n - 8, 1), pltpu.roll(x, 8, 1))


def _mla_proj_kernel(x_ref, g_ref, mod_ref, wd_ref, qn_ref, kvn_ref, wuq_ref, wukv_ref, cos_ref, sin_ref,
                     q_ref, kv_ref, kr_ref):
    h = _norm_mod(x_ref[...], g_ref[...], mod_ref, 0, 1).astype(BF16)
    dq = _dot(h, wd_ref[...])
    cq = dq[:, :MLA_Q_LORA]
    ckv = dq[:, MLA_Q_LORA:MLA_Q_LORA + MLA_KV_LORA]
    kr = dq[:, MLA_Q_LORA + MLA_KV_LORA:]

    def rms(x, w):
        return x * lax.rsqrt(jnp.mean(x * x, axis=-1, keepdims=True) + NORM_EPS) * w

    q = _dot(rms(cq, qn_ref[...]).astype(BF16), wuq_ref[...])
    kv_ref[...] = _dot(rms(ckv, kvn_ref[...]).astype(BF16), wukv_ref[...]).astype(BF16)
    cos = cos_ref[...]
    sin = sin_ref[...]
    lane = lax.broadcasted_iota(jnp.int32, (1, LANES), 1)
    kr_ref[...] = (kr * cos + _rope_rot(kr, lane) * sin).astype(BF16)
    lane_q = lax.broadcasted_iota(jnp.int32, (1, q.shape[1]), 1) & (LANES - 1)
    reps = q.shape[1] // LANES
    cos_q = jnp.concatenate([cos] * reps, axis=1)
    sin_q = jnp.concatenate([sin] * reps, axis=1)
    q_ref[...] = ((q * cos_q + _rope_rot(q, lane_q) * sin_q) * (MLA_SCALE * LOG2E)).astype(BF16)


def _mla_proj(x, g, mod, p, cos_t, sin_t, *, L, Lc, B, tm=512):
    nt, d = x.shape
    n_lat_tiles = B * L // tm
    per_seq = L // tm
    nq = MLA_HEADS * LANES

    def full(a):
        nd = a.ndim
        return pl.BlockSpec(a.shape, lambda i: (0,) * nd)

    def tab(i):
        return (jnp.where(i < n_lat_tiles, i % per_seq, per_seq), 0)

    weights = [p["wd"], p["q_norm"], p["kv_norm"], p["wuq"], p["wukv"]]
    return pl.pallas_call(
        _mla_proj_kernel,
        grid=(nt // tm,),
        in_specs=[
            pl.BlockSpec((tm, d), lambda i: (i, 0)),
            pl.BlockSpec((1, d), lambda i: (0, 0)),
            pl.BlockSpec((1, 6, d), _mod_index(tm, L, B)),
        ] + [full(w) for w in weights] + [pl.BlockSpec((tm, LANES), tab), pl.BlockSpec((tm, LANES), tab)],
        out_specs=[
            pl.BlockSpec((tm, nq), lambda i: (i, 0)),
            pl.BlockSpec((tm, nq), lambda i: (i, 0)),
            pl.BlockSpec((tm, LANES), lambda i: (i, 0)),
        ],
        out_shape=[
            jax.ShapeDtypeStruct((nt, nq), BF16),
            jax.ShapeDtypeStruct((nt, nq), BF16),
            jax.ShapeDtypeStruct((nt, LANES), BF16),
        ],
        compiler_params=_params(1),
        name="mla_proj",
    )(x, g.reshape(1, d), mod, *weights, cos_t, sin_t)


def _attn_kernel(*refs, has_lat):
    if has_lat:
        q_ref, kvc_ref, krc_ref, kvl_ref, krl_ref, o_ref = refs
    else:
        q_ref, kvc_ref, krc_ref, o_ref = refs
    lane = lax.broadcasted_iota(jnp.int32, (1, LANES), 1)
    m0 = lane < MLA_NOPE
    heads = range(2)
    cols = [slice(hh * LANES, (hh + 1) * LANES) for hh in heads]
    q = [q_ref[:, cols[hh]] for hh in heads]
    kvs = [[kvc_ref[:, cols[hh]] for hh in heads]]
    krs = [krc_ref[...]]
    if has_lat:
        kvs.append([kvl_ref[:, cols[hh]] for hh in heads])
        krs.append(krl_ref[...])
    st = [[_dot_nt(jnp.where(m0, kv[hh], kr), q[hh]) for kv, kr in zip(kvs, krs)] for hh in heads]
    outs = []
    for hh in heads:
        m = st[hh][0].max(axis=0, keepdims=True)
        for s_ in st[hh][1:]:
            m = jnp.maximum(m, s_.max(axis=0, keepdims=True))
        den = None
        acc = None
        for part, s_ in enumerate(st[hh]):
            p = jnp.exp2(s_ - m)
            ps = p.sum(axis=0, keepdims=True)
            den = ps if den is None else den + ps
            o = _dot_tn(kvs[part][hh], p.astype(BF16))
            acc = o if acc is None else acc + o
        outs.append((acc / den)[MLA_NOPE:, :])
    o_ref[...] = jnp.concatenate(outs, axis=0).T.astype(BF16)


def _attention(q, kv, kr, *, B, L, Lc, latent, tq=512):
    npair = MLA_HEADS // 2
    ctx0 = B * L // Lc
    w2 = 2 * LANES
    kv_ctx = pl.BlockSpec((Lc, w2), lambda b, h, t: (ctx0 + b, h))
    kr_ctx = pl.BlockSpec((Lc, LANES), lambda b, h, t: (ctx0 + b, 0))
    if latent:
        nq_t = L // tq
        in_specs = [
            pl.BlockSpec((tq, w2), lambda b, h, t: (b * nq_t + t, h)),
            kv_ctx, kr_ctx,
            pl.BlockSpec((L, w2), lambda b, h, t: (b, h)),
            pl.BlockSpec((L, LANES), lambda b, h, t: (b, 0)),
        ]
        args = (q, kv, kr, kv, kr)
        rows = B * L
        out_spec = pl.BlockSpec((tq, LANES), lambda b, h, t: (b * nq_t + t, h))
    else:
        nq_t = 1
        in_specs = [pl.BlockSpec((Lc, w2), lambda b, h, t: (ctx0 + b, h)), kv_ctx, kr_ctx]
        args = (q, kv, kr)
        rows = B * Lc
        out_spec = pl.BlockSpec((Lc, LANES), lambda b, h, t: (b, h))
    return pl.pallas_call(
        functools.partial(_attn_kernel, has_lat=latent),
        grid=(B, npair, nq_t),
        in_specs=in_specs,
        out_specs=out_spec,
        out_shape=jax.ShapeDtypeStruct((rows, npair * LANES), BF16),
        compiler_params=_params(3),
        name="mla_attention_lat" if latent else "mla_attention_ctx",
    )(*args)


def _rope_tables(L, tm):
    n_rows = L // GRID_W
    row = jnp.repeat(jnp.arange(n_rows, dtype=F32), GRID_W)
    col = jnp.tile(jnp.arange(GRID_W, dtype=F32), n_rows)
    nq = MLA_ROPE // 4
    inv_freq = ROPE_BASE ** (-jnp.arange(nq, dtype=F32) / nq)
    ang_r = row[:, None] * inv_freq
    ang_c = col[:, None] * inv_freq
    ang = jnp.concatenate([ang_r, ang_r, ang_c, ang_c], axis=-1)
    cos = jnp.ones((L + tm, LANES), F32).at[:L, MLA_NOPE:MLA_NOPE + MLA_ROPE].set(jnp.cos(ang))
    sin = jnp.zeros((L + tm, LANES), F32).at[:L, MLA_NOPE:MLA_NOPE + MLA_ROPE].set(jnp.sin(ang))
    return cos, sin


def kernel(x, c, ctx, c_ctx, w_mod, b_mod, norm1, norm2, ffn_w_in, ffn_conv, ffn_conv_b, ffn_w_out, hg_w_in, hg_lb, hg_norm, hg_w_o, rw_mu, rw_w_rkv, rw_w0, rw_w1, rw_w2, rw_a0, rw_a1, rw_a2, rw_g1, rw_g2, rw_k_k, rw_k_a, rw_r_k, rw_ln_w, rw_ln_b, rw_w_o, mla_w_dqkv, mla_q_norm, mla_kv_norm, mla_w_uq, mla_w_ukv, mla_w_o, norm_f):
    B, L, D = x.shape
    Lc = ctx.shape[1]
    depth = w_mod.shape[0]
    assert D == D_MODEL and L & (L - 1) == 0 and Lc & (Lc - 1) == 0 and L % 2048 == 0 and (B * Lc) % 2048 == 0
    n_lat = B * L
    nt = n_lat + B * Lc

    X = jnp.concatenate([x.reshape(n_lat, D), ctx.reshape(B * Lc, D)], axis=0)

    mod_rows = 8 * ((B + 1 + 7) // 8)
    cc = jnp.zeros((mod_rows, D), F32).at[:B].set(c).at[B].set(c_ctx)
    mod_all = _modulation(cc, w_mod, b_mod)[:, :B + 1].reshape(depth, B + 1, 6, D)

    lb_p = jnp.cumsum(jax.nn.softmax(hg_lb.astype(F32), axis=1), axis=1)
    lower_bounds = lb_p - lb_p[:, :1]

    for layer in range(depth):
        last = layer == depth - 1
        kind, j = layer % N_MIXERS, layer // N_MIXERS
        mod = mod_all[layer]
        n_rows = n_lat if last else nt
        kw = dict(L=L, B=B)
        if kind == 0:
            u = _norm_mod_matmul(X, norm1[layer], mod, hg_w_in[j].astype(BF16), shift_i=0, scale_i=1, **kw)
            y_l, y_c = _hgrn_scan(u, lower_bounds[:, j], hg_norm[j], B=B, L=L, Lc=Lc)
            w_o = hg_w_o[j]
        elif kind == 1:
            p = dict(mu=rw_mu[j], w_rkv=rw_w_rkv[j].astype(BF16), w0=rw_w0[j], w1=rw_w1[j].astype(BF16),
                     w2=rw_w2[j].astype(BF16), a0=rw_a0[j], a1=rw_a1[j].astype(BF16), a2=rw_a2[j].astype(BF16),
                     g1=rw_g1[j].astype(BF16), g2=rw_g2[j].astype(BF16),
                     k_k=rw_k_k[j], k_a=rw_k_a[j], r_k=rw_r_k[j].reshape(D), ln_w=rw_ln_w[j], ln_b=rw_ln_b[j])
            r, k, v, g, ld, a = _rwkv_proj(X, norm1[layer], mod, p, L=L, Lc=Lc, B=B)
            y_l, y_c = _rwkv_scan(r, k, v, g, ld, a, p, B=B, L=L, Lc=Lc)
            w_o = rw_w_o[j]
        else:
            nlq = MLA_Q_LORA + MLA_KV_LORA
            wd = jnp.zeros((D, nlq + LANES), F32).at[:, :nlq].set(mla_w_dqkv[j][:, :nlq])
            wd = wd.at[:, nlq + MLA_NOPE:nlq + MLA_NOPE + MLA_ROPE].set(mla_w_dqkv[j][:, nlq:])
            wuq = mla_w_uq[j].reshape(MLA_Q_LORA, MLA_HEADS, MLA_NOPE + MLA_ROPE)
            wuq = jnp.pad(wuq, ((0, 0), (0, 0), (0, LANES - MLA_NOPE - MLA_ROPE))).reshape(MLA_Q_LORA, MLA_HEADS * LANES)
            p = dict(wd=wd.astype(BF16), q_norm=mla_q_norm[j].reshape(1, -1), kv_norm=mla_kv_norm[j].reshape(1, -1),
                     wuq=wuq.astype(BF16), wukv=mla_w_ukv[j].astype(BF16))
            tm_p = 512
            cos_t, sin_t = _rope_tables(L, tm_p)
            q, kv, kr = _mla_proj(X, norm1[layer], mod, p, cos_t, sin_t, L=L, Lc=Lc, B=B, tm=tm_p)
            y_l = _attention(q, kv, kr, B=B, L=L, Lc=Lc, latent=True)
            y_c = _attention(q, kv, kr, B=B, L=L, Lc=Lc, latent=False)
            w_o = mla_w_o[j]
        y = y_l if last else (y_l, y_c)
        X = _matmul_residual(y, w_o.astype(BF16), X, mod, gate_i=2, n_rows=n_rows, **kw)
        gmid = _ffn_gate(X, norm2[layer], mod, ffn_w_in[layer].astype(BF16), ffn_conv[layer], ffn_conv_b[layer],
                         L=L, Lc=Lc, B=B, n_rows=n_rows)
        X = _matmul_residual(gmid, ffn_w_out[layer].astype(BF16), X, mod, gate_i=5, n_rows=n_rows,
                             norm_f=norm_f if last else None, **kw)
    return X.reshape(B, L, D)
```

```python
import functools

import jax
import jax.numpy as jnp
from jax import lax
from jax.experimental import pallas as pl
from jax.experimental.pallas import tpu as pltpu

F32 = jnp.float32
BF16 = jnp.bfloat16

D_MODEL = 1024
N_MIXERS = 3
NORM_EPS = 1e-6
HG_HEADS = 8
HG_CHUNK = 64
RW_HEAD = 64
RW_LN_EPS = 64e-5
RW_CHUNK = 64
MLA_HEADS = 16
MLA_NOPE = 64
MLA_ROPE = 32
MLA_V = 64
MLA_Q_LORA = 256
MLA_KV_LORA = 256
MLA_SCALE = (MLA_NOPE + MLA_ROPE) ** -0.5
ROPE_BASE = 10000.0
LOG2E = 1.4426950408889634
GRID_W = 64
LANES = 128
VMEM_LIMIT = 56 * 1024 * 1024


def _params(n_axes):
    return pltpu.CompilerParams(dimension_semantics=("arbitrary",) * n_axes, vmem_limit_bytes=VMEM_LIMIT)


def _dot(a, b):
    return jnp.dot(a, b, preferred_element_type=F32)


def _dot_nt(a, b):
    return lax.dot_general(a, b, (((1,), (1,)), ((), ())), preferred_element_type=F32)


def _dot_tn(a, b):
    return lax.dot_general(a, b, (((0,), (0,)), ((), ())), preferred_element_type=F32)


def _split(x):
    hi = x.astype(BF16)
    lo = (x - hi.astype(F32)).astype(BF16)
    return hi, lo


def _dot_exact_lhs(m_bf16, x):
    hi, lo = _split(x)
    return _dot(m_bf16, hi) + _dot(m_bf16, lo)


def _dot_exact_rhs(x, m_bf16):
    hi, lo = _split(x)
    return _dot(hi, m_bf16) + _dot(lo, m_bf16)


def _silu(x):
    return x * jax.nn.sigmoid(x)


def _norm_mod(x, g, mod_ref, shift_i, scale_i):
    ms = jnp.mean(x * x, axis=-1, keepdims=True)
    y = x * lax.rsqrt(ms + NORM_EPS) * g
    return y * (1.0 + mod_ref[0, scale_i:scale_i + 1, :]) + mod_ref[0, shift_i:shift_i + 1, :]


def _mod_index(tm, L, B):
    return lambda i, *_: (jnp.minimum(i * tm // L, B), 0, 0)


def _mod_kernel(c_ref, w_ref, b_ref, o_ref):
    s = _silu(c_ref[...]).astype(BF16)
    o_ref[0] = _dot(s, w_ref[0].astype(BF16)) + b_ref[0]


def _modulation(cc, w_mod, b_mod):
    depth, d, n = w_mod.shape
    rows = cc.shape[0]
    tn = 1024
    return pl.pallas_call(
        _mod_kernel,
        grid=(depth, n // tn),
        in_specs=[
            pl.BlockSpec((rows, d), lambda l, j: (0, 0)),
            pl.BlockSpec((1, d, tn), lambda l, j: (l, 0, j)),
            pl.BlockSpec((1, 1, tn), lambda l, j: (l, 0, j)),
        ],
        out_specs=pl.BlockSpec((1, rows, tn), lambda l, j: (l, 0, j)),
        out_shape=jax.ShapeDtypeStruct((depth, rows, n), F32),
        compiler_params=_params(2),
        name="modulation",
    )(cc, w_mod, b_mod.reshape(depth, 1, n))


def _nmm_kernel(x_ref, g_ref, mod_ref, w_ref, o16_ref, o32_ref, h_scr, *, shift_i, scale_i, n16):
    j = pl.program_id(1)

    @pl.when(j == 0)
    def _():
        h_scr[...] = _norm_mod(x_ref[...], g_ref[...], mod_ref, shift_i, scale_i).astype(BF16)

    acc = _dot(h_scr[...], w_ref[...])

    def store(o_ref):
        for s in range(o_ref.shape[0]):
            o_ref[s] = acc[:, s * LANES:(s + 1) * LANES].astype(o_ref.dtype)

    @pl.when(j < n16)
    def _():
        store(o16_ref)

    @pl.when(j >= n16)
    def _():
        store(o32_ref)


def _norm_mod_matmul(x, g, mod, w, *, n_bf16, shift_i, scale_i, L, B, tm=1024, tn=512):
    nt, d = x.shape
    n = w.shape[1]
    spt = tn // LANES
    n16 = n_bf16 // tn
    assert n_bf16 % tn == 0 and 0 < n16 < n // tn
    return pl.pallas_call(
        functools.partial(_nmm_kernel, shift_i=shift_i, scale_i=scale_i, n16=n16),
        grid=(nt // tm, n // tn),
        in_specs=[
            pl.BlockSpec((tm, d), lambda i, j: (i, 0)),
            pl.BlockSpec((1, d), lambda i, j: (0, 0)),
            pl.BlockSpec((1, 6, d), _mod_index(tm, L, B)),
            pl.BlockSpec((d, tn), lambda i, j: (0, j)),
        ],
        out_specs=[
            pl.BlockSpec((spt, tm, LANES), lambda i, j: (jnp.minimum(j, n16 - 1), i, 0)),
            pl.BlockSpec((spt, tm, LANES), lambda i, j: (jnp.maximum(j - n16, 0), i, 0)),
        ],
        out_shape=[
            jax.ShapeDtypeStruct((n_bf16 // LANES, nt, LANES), BF16),
            jax.ShapeDtypeStruct(((n - n_bf16) // LANES, nt, LANES), F32),
        ],
        scratch_shapes=[pltpu.VMEM((tm, d), BF16)],
        compiler_params=_params(2),
        name="norm_mod_matmul",
    )(x, g.reshape(1, d), mod, w)


def _mmres_kernel(*refs, gate_i, final, n_lat_tiles):
    if n_lat_tiles is None:
        a_ref, w_ref, x_ref, mod_ref, *rest = refs
        if len(a_ref.shape) == 3:
            a = jnp.concatenate([a_ref[c] for c in range(a_ref.shape[0])], axis=1)
        else:
            a = a_ref[...]
    else:
        al_ref, ac_ref, w_ref, x_ref, mod_ref, *rest = refs
        a = jnp.where(pl.program_id(0) < n_lat_tiles, al_ref[...], ac_ref[...])
    y = x_ref[...] + mod_ref[0, gate_i:gate_i + 1, :] * _dot(a, w_ref[...])
    if final:
        nf_ref, o_ref = rest
        ms = jnp.mean(y * y, axis=-1, keepdims=True)
        y = y * lax.rsqrt(ms + NORM_EPS) * nf_ref[...]
    else:
        (o_ref,) = rest
    o_ref[...] = y


def _matmul_residual(a, w, x, mod, *, gate_i, L, B, n_rows, tm=512, norm_f=None):
    split = isinstance(a, tuple)
    blocked = not split and a.ndim == 3
    k, d = w.shape
    final = norm_f is not None
    if split:
        n_lat_tiles = a[0].shape[0] // tm
        a_specs = [
            pl.BlockSpec((tm, k), lambda i: (jnp.minimum(i, n_lat_tiles - 1), 0)),
            pl.BlockSpec((tm, k), lambda i: (jnp.maximum(i - n_lat_tiles, 0), 0)),
        ]
        a_args = list(a)
    else:
        n_lat_tiles = None
        if blocked:
            a_specs = [pl.BlockSpec((a.shape[0], tm, a.shape[2]), lambda i: (0, i, 0))]
        else:
            a_specs = [pl.BlockSpec((tm, k), lambda i: (i, 0))]
        a_args = [a]
    in_specs = a_specs + [
        pl.BlockSpec((k, d), lambda i: (0, 0)),
        pl.BlockSpec((tm, d), lambda i: (i, 0)),
        pl.BlockSpec((1, 6, d), _mod_index(tm, L, B)),
    ]
    args = a_args + [w, x, mod]
    if final:
        in_specs.append(pl.BlockSpec((1, d), lambda i: (0, 0)))
        args.append(norm_f.reshape(1, d))
    return pl.pallas_call(
        functools.partial(_mmres_kernel, gate_i=gate_i, final=final, n_lat_tiles=n_lat_tiles),
        grid=(n_rows // tm,),
        in_specs=in_specs,
        out_specs=pl.BlockSpec((tm, d), lambda i: (i, 0)),
        out_shape=jax.ShapeDtypeStruct((n_rows, d), F32),
        compiler_params=_params(1),
        name="matmul_residual",
    )(*args)


def _ffn1_kernel(x_ref, g_ref, mod_ref, wa_ref, wv_ref, cw_ref, cb_ref, o_ref, h_scr, *, tm, rb, n_lat_tiles, L, Lc):
    i = pl.program_id(0)

    @pl.when(pl.program_id(1) == 0)
    def _():
        h_scr[...] = _norm_mod(x_ref[...], g_ref[...], mod_ref, 3, 4).astype(BF16)

    is_lat = i < n_lat_tiles
    sub = lax.broadcasted_iota(jnp.int32, (8, 1), 0)
    nb = tm // rb
    a_blk, v_blk = [], []
    zero_row = jnp.zeros((1, wa_ref.shape[1]), F32)

    def halo(r0, nbr_row):
        if r0 % tm == 0:
            return zero_row
        edge = jnp.where(is_lat, r0 % L == 0, r0 % Lc == 0)
        return jnp.where(edge, 0.0, nbr_row)

    def gate(r):
        a = a_blk[r]
        up = halo(r * rb, a_blk[r - 1][rb - 1:rb, :] if r > 0 else zero_row)
        dn = halo((r + 1) * rb, a_blk[r + 1][0:1, :] if r < nb - 1 else zero_row)
        prev = pltpu.roll(a, 1, 0)
        prev = jnp.concatenate([jnp.where(sub == 0, up, prev[0:8, :]), prev[8:, :]], axis=0)
        nxt = pltpu.roll(a, rb - 1, 0)
        nxt = jnp.concatenate([nxt[:rb - 8, :], jnp.where(sub == 7, dn, nxt[rb - 8:, :])], axis=0)
        ac = cw_ref[0:1, :] * prev + cw_ref[1:2, :] * a + cw_ref[2:3, :] * nxt + cb_ref[...]
        o_ref[r * rb:(r + 1) * rb, :] = (_silu(ac) * v_blk[r]).astype(BF16)

    for r in range(nb):
        h = h_scr[r * rb:(r + 1) * rb, :]
        a_blk.append(_dot(h, wa_ref[...]))
        v_blk.append(_dot(h, wv_ref[...]))
        if r > 0:
            gate(r - 1)
    gate(nb - 1)


def _ffn_gate(x, g, mod, w_in, conv_w, conv_b, *, L, Lc, B, n_rows, tm=2048, tf=256, rb=256):
    assert L % rb == 0 and Lc % rb == 0 and tm % rb == 0
    d = x.shape[1]
    dff = conv_w.shape[1]
    nf = dff // tf
    w_blk = w_in.reshape(d, 2 * nf, tf).transpose(1, 0, 2)
    return pl.pallas_call(
        functools.partial(_ffn1_kernel, tm=tm, rb=rb, n_lat_tiles=B * L // tm, L=L, Lc=Lc),
        grid=(n_rows // tm, nf),
        in_specs=[
            pl.BlockSpec((tm, d), lambda i, j: (i, 0)),
            pl.BlockSpec((1, d), lambda i, j: (0, 0)),
            pl.BlockSpec((1, 6, d), _mod_index(tm, L, B)),
            pl.BlockSpec((None, d, tf), lambda i, j: (j, 0, 0)),
            pl.BlockSpec((None, d, tf), lambda i, j: (j + nf, 0, 0)),
            pl.BlockSpec((3, tf), lambda i, j: (0, j)),
            pl.BlockSpec((1, tf), lambda i, j: (0, j)),
        ],
        out_specs=pl.BlockSpec((None, tm, tf), lambda i, j: (j, i, 0)),
        out_shape=jax.ShapeDtypeStruct((nf, n_rows, tf), BF16),
        scratch_shapes=[pltpu.VMEM((tm, d), BF16)],
        compiler_params=_params(2),
        name="ffn_gate",
    )(x, g.reshape(1, d), mod, w_blk, w_blk, conv_w, conv_b.reshape(1, dff))


def _tri(n, rev):
    t = lax.broadcasted_iota(jnp.int32, (n, n), 0)
    s = lax.broadcasted_iota(jnp.int32, (n, n), 1)
    return (s >= t) if rev else (s <= t)


def _gla_group(q, k, v, lf, rev):
    n = len(q)
    c = q[0].shape[0]
    mid = c // 2
    tri = {r: _tri(c, r) for r in set(rev)}
    tri_b = {r: m.astype(BF16) for r, m in tri.items()}
    mid_row = [(c - 1 - mid) if rev[i] else mid for i in range(n)]
    end_row = [0 if rev[i] else c - 1 for i in range(n)]
    b = [_dot_exact_lhs(tri_b[rev[i]], lf[i]) for i in range(n)]
    b_mid = [b[i][mid_row[i]:mid_row[i] + 1, :] for i in range(n)]
    b_end = [b[i][end_row[i]:end_row[i] + 1, :] for i in range(n)]
    qs = [(q[i] * jnp.exp(b[i] - b_mid[i])).astype(BF16) for i in range(n)]
    ks = [(k[i] * jnp.exp(b_mid[i] - b[i])).astype(BF16) for i in range(n)]
    att = [_dot_nt(qs[i], ks[i]) for i in range(n)]
    att = [jnp.where(tri[rev[i]], att[i], 0.0).astype(BF16) for i in range(n)]
    vb = [v[i].astype(BF16) for i in range(n)]
    o_intra = [_dot(att[i], vb[i]) for i in range(n)]
    kd = [(k[i] * jnp.exp(b_end[i] - b[i])).astype(BF16) for i in range(n)]
    ds = [_dot_tn(vb[i], kd[i]) for i in range(n)]
    qd = [(q[i] * jnp.exp(b[i])).astype(BF16) for i in range(n)]
    e_end = [jnp.exp(b_end[i]) for i in range(n)]
    return o_intra, qd, ds, e_end


def _hgrn_kernel(ql, il, gl, zfl, zbl, qc, ic, gc, zfc, zbc, lb_ref, gn_ref, ol_ref, oc_ref,
                 of_l, ob_l, of_c, ob_c, sf, sb):
    C = HG_CHUNK
    lbf = lb_ref[0, 0]
    lbb = lb_ref[1, 0]

    def gates(z, lb):
        return jnp.log(lb + (1.0 - lb) * jax.nn.sigmoid(z)), (1.0 - lb) * jax.nn.sigmoid(-z)

    def part(q_ref, i_ref, zf_ref, zb_ref, of_ref, ob_ref, U):
        n = q_ref.shape[0] // C

        def body(t, carry):
            rows = [pl.ds(pl.multiple_of((t * U + u) * C, C), C) for u in range(U)]
            rows += [pl.ds(pl.multiple_of((n - 1 - t * U - u) * C, C), C) for u in range(U)]
            rev = [False] * U + [True] * U
            zs = [(zb_ref if r else zf_ref)[rw, :] for rw, r in zip(rows, rev)]
            gk = [gates(z, lbb if r else lbf) for z, r in zip(zs, rev)]
            q = [_silu(q_ref[rw, :].astype(F32)) for rw in rows]
            v = [i_ref[rw, :] for rw in rows]
            o_intra, qd, ds, e_end = _gla_group(q, [g[1] for g in gk], v, [g[0] for g in gk], rev)
            st = [sf[...], sb[...]]
            for u in range(U):
                for d in range(2):
                    i = d * U + u
                    o = o_intra[i] + _dot_nt(qd[i], st[d].astype(BF16))
                    (ob_ref if d else of_ref)[rows[i], :] = o
                    st[d] = st[d] * e_end[i] + ds[i]
            sf[...] = st[0]
            sb[...] = st[1]
            return carry

        lax.fori_loop(0, n // U, body, 0)

    sf[...] = jnp.zeros_like(sf)
    sb[...] = jnp.zeros_like(sb)
    part(qc, ic, zfc, zbc, of_c, ob_c, 4)
    part(ql, il, zfl, zbl, of_l, ob_l, 8)

    def readout(of_ref, ob_ref, g_ref, o_ref):
        o = of_ref[...] + ob_ref[...]
        ms = jnp.mean(o * o, axis=-1, keepdims=True)
        y = o * lax.rsqrt(ms + NORM_EPS) * gn_ref[...]
        o_ref[...] = (y * _silu(g_ref[...].astype(F32))).astype(BF16)

    readout(of_c, ob_c, gc, oc_ref)
    readout(of_l, ob_l, gl, ol_ref)


def _hgrn_scan(u16, u32, lbs, g_norm, *, B, L, Lc):
    H = HG_HEADS
    ctx0 = B * L // Lc

    def lat(sec):
        return pl.BlockSpec((None, L, LANES), lambda b, h: (sec * H + h, b, 0))

    def ctx(sec):
        return pl.BlockSpec((None, Lc, LANES), lambda b, h: (sec * H + h, ctx0 + b, 0))

    secs = (0, 1, 2, 0, 1)
    arrays = [u16, u16, u16, u32, u32]
    return pl.pallas_call(
        _hgrn_kernel,
        grid=(B, H),
        in_specs=[lat(s) for s in secs] + [ctx(s) for s in secs] + [
            pl.BlockSpec((2, 1, 1, LANES), lambda b, h: (0, h, 0, 0)),
            pl.BlockSpec((1, LANES), lambda b, h: (0, 0)),
        ],
        out_specs=[
            pl.BlockSpec((L, LANES), lambda b, h: (b, h)),
            pl.BlockSpec((Lc, LANES), lambda b, h: (b, h)),
        ],
        out_shape=[
            jax.ShapeDtypeStruct((B * L, H * LANES), BF16),
            jax.ShapeDtypeStruct((B * Lc, H * LANES), BF16),
        ],
        scratch_shapes=[
            pltpu.VMEM((L, LANES), F32), pltpu.VMEM((L, LANES), F32),
            pltpu.VMEM((Lc, LANES), F32), pltpu.VMEM((Lc, LANES), F32),
            pltpu.VMEM((LANES, LANES), F32), pltpu.VMEM((LANES, LANES), F32),
        ],
        compiler_params=_params(2),
        name="hgrn_scan",
    )(*(arrays * 2), lbs.reshape(2, H, 1, LANES), g_norm.reshape(1, LANES))


def _softplus(x):
    return jnp.maximum(x, 0.0) + jnp.log1p(jnp.exp(-jnp.abs(x)))


def _rwkv_proj_kernel(x_ref, xp_ref, xn_ref, g_ref, mod_ref, mu_ref, wrkv_ref, w0_ref, w1_ref, w2_ref,
                      a0_ref, a1_ref, a2_ref, g1_ref, g2_ref,
                      r_ref, k_ref, v_ref, gg_ref, ld_ref, a_ref, *, tm, n_lat_tiles, L, Lc):
    i = pl.program_id(0)
    g = g_ref[...]
    h = _norm_mod(x_ref[...], g, mod_ref, 0, 1)
    hp = _norm_mod(xp_ref[...], g, mod_ref, 0, 1)[7:8, :]
    hn = _norm_mod(xn_ref[...], g, mod_ref, 0, 1)[0:1, :]
    seqlen = jnp.where(i < n_lat_tiles, L, Lc)
    row = lax.broadcasted_iota(jnp.int32, (tm, 1), 0)
    pos = (i * tm + row) & (seqlen - 1)
    down = jnp.where(row == 0, hp, pltpu.roll(h, 1, 0))
    down = jnp.where(pos == 0, 0.0, down)
    up = jnp.where(row == tm - 1, hn, pltpu.roll(h, tm - 1, 0))
    up = jnp.where(pos == seqlen - 1, 0.0, up)
    dx = 0.5 * (down + up) - h

    def mix(j):
        return (h + dx * mu_ref[j:j + 1, :]).astype(BF16)

    xw, xa, xg = mix(1), mix(4), mix(5)
    t_w = [_dot(xw, w1_ref[d]) for d in range(2)]
    t_a = [_dot(xa, a1_ref[d]) for d in range(2)]
    t_g = _dot(xg, g1_ref[...])
    wl = [_dot(jnp.tanh(t_w[d]).astype(BF16), w2_ref[d]) for d in range(2)]
    al = [_dot(t_a[d].astype(BF16), a2_ref[d]) for d in range(2)]
    gg_ref[...] = _dot(jax.nn.sigmoid(t_g).astype(BF16), g2_ref[...])
    r = _dot(mix(0), wrkv_ref[0])
    k = _dot(mix(2), wrkv_ref[1])
    v = _dot(mix(3), wrkv_ref[2])
    for d in range(2):
        w_log = -_softplus(-(w0_ref[d:d + 1, :] + wl[d])) - 0.5
        ld_ref[d] = -jnp.exp(w_log)
        a_ref[d] = jax.nn.sigmoid(a0_ref[d:d + 1, :] + al[d])
    r_ref[...] = r
    k_ref[...] = k
    v_ref[...] = v


def _rwkv_proj(x, g, mod, p, *, L, Lc, B, tm=256):
    nt, d = x.shape
    blk8 = tm // 8

    def full(a):
        nd = a.ndim
        return pl.BlockSpec(a.shape, lambda i: (0,) * nd)

    weights = [p["mu"], p["w_rkv"], p["w0"], p["w1"], p["w2"], p["a0"], p["a1"], p["a2"], p["g1"], p["g2"]]
    row_spec = pl.BlockSpec((tm, d), lambda i: (i, 0))
    dir_spec = pl.BlockSpec((2, tm, d), lambda i: (0, i, 0))
    return pl.pallas_call(
        functools.partial(_rwkv_proj_kernel, tm=tm, n_lat_tiles=B * L // tm, L=L, Lc=Lc),
        grid=(nt // tm,),
        in_specs=[
            row_spec,
            pl.BlockSpec((8, d), lambda i: (jnp.maximum(i * blk8 - 1, 0), 0)),
            pl.BlockSpec((8, d), lambda i: (jnp.minimum((i + 1) * blk8, nt // 8 - 1), 0)),
            pl.BlockSpec((1, d), lambda i: (0, 0)),
            pl.BlockSpec((1, 6, d), _mod_index(tm, L, B)),
        ] + [full(w) for w in weights],
        out_specs=[row_spec, row_spec, row_spec, row_spec, dir_spec, dir_spec],
        out_shape=[jax.ShapeDtypeStruct((nt, d), F32)] * 4 + [jax.ShapeDtypeStruct((2, nt, d), F32)] * 2,
        compiler_params=_params(1),
        name="rwkv_proj",
    )(x, x, x, g.reshape(1, d), mod, *weights)


def _rwkv_scan_kernel(rl, kl, vl, gl, ldl, al, rc, kc, vc, gc, ldc, ac,
                      kk_ref, ka_ref, rk_ref, lnw_ref, lnb_ref, ol_ref, oc_ref,
                      G, Hs, Q, Y1, Yo, z_scr):
    C = RW_CHUNK
    C2 = 2 * C
    Lc = rc.shape[0]
    L = rl.shape[0]
    nc, nl = Lc // C, L // C
    lane = lax.broadcasted_iota(jnp.int32, (1, LANES), 1)
    m0 = lane < RW_HEAD
    ri = lax.broadcasted_iota(jnp.int32, (LANES, LANES), 0)
    ci = lax.broadcasted_iota(jnp.int32, (LANES, LANES), 1)
    same_head = (ri < RW_HEAD) == (ci < RW_HEAD)
    eye = ri == ci
    ones_bd = same_head.astype(BF16)
    k_k = kk_ref[...]
    k_a = ka_ref[...]

    def segsum(x):
        return _dot_exact_rhs(x, ones_bd)

    def x2(x):
        return jnp.concatenate([jnp.where(m0, x, 0.0), jnp.where(m0, 0.0, x)], axis=0)

    def t2(x):
        return jnp.concatenate([x, x], axis=0)

    def sel(x):
        return jnp.where(m0, x[:C, :], x[C:, :])

    tt = lax.broadcasted_iota(jnp.int32, (C2, C2), 0)
    ss = lax.broadcasted_iota(jnp.int32, (C2, C2), 1)
    sh = (tt < C) == (ss < C)
    tl, sl = tt & (C - 1), ss & (C - 1)
    strict = {False: sh & (sl < tl), True: sh & (sl > tl)}
    incl = {False: sh & (sl <= tl), True: sh & (sl >= tl)}
    eye2 = jnp.where(tt == ss, 1.0, 0.0)
    tri_b = {r: _tri(C, r).astype(BF16) for r in (False, True)}

    def phase1(insts):
        n = range(len(insts))
        rev = [d == 1 for (_, d, _, _) in insts]
        rows = [pl.ds(r0, C) for (_, _, r0, _) in insts]
        r = [insts[i][0][0][rows[i], :] for i in n]
        k = [insts[i][0][1][rows[i], :] for i in n]
        v = [insts[i][0][2][rows[i], :] for i in n]
        lw = [insts[i][0][3][insts[i][1], rows[i], :] for i in n]
        a = [insts[i][0][4][insts[i][1], rows[i], :] for i in n]
        kk = [k[i] * k_k for i in n]
        n2 = [segsum(kk[i] * kk[i]) for i in n]
        yield
        kk = [kk[i] / jnp.maximum(jnp.sqrt(n2[i]), 1e-12) for i in n]
        kmod = [k[i] * (1.0 + (a[i] - 1.0) * k_a) for i in n]
        beta = [kk[i] * a[i] for i in n]
        c_in = [_dot_exact_lhs(tri_b[rev[i]], lw[i]) for i in n]
        yield
        c_end = [c_in[i][(0 if rev[i] else C - 1):(0 if rev[i] else C - 1) + 1, :] for i in n]
        e_neg = [jnp.exp(-c_in[i]) for i in n]
        e_end = [jnp.exp(c_end[i] - c_in[i]) for i in n]
        ea = [-kk[i] * jnp.exp(c_in[i] - lw[i]) for i in n]
        rb = [r[i] * jnp.exp(c_in[i]) for i in n]
        lhs = [jnp.concatenate([x2(ea[i]), x2(rb[i])], axis=0).astype(BF16) for i in n]
        rhs = [jnp.concatenate([t2(beta[i] * e_neg[i]), t2(kmod[i] * e_neg[i])], axis=0).astype(BF16) for i in n]
        M = [_dot_nt(lhs[i], rhs[i]) for i in n]
        yield
        A = [jnp.where(strict[rev[i]], M[i][:C2, :C2], 0.0) for i in n]
        Bm = [jnp.where(strict[rev[i]], M[i][:C2, C2:], 0.0).astype(BF16) for i in n]
        arbr = [jnp.concatenate([jnp.where(incl[rev[i]], M[i][C2:, :C2], 0.0),
                                 jnp.where(incl[rev[i]], M[i][C2:, C2:], 0.0)], axis=1).astype(BF16) for i in n]
        T = [eye2 + A[i] for i in n]
        P = A
        for _ in range(C.bit_length() - 2):
            Pb = [P[i].astype(BF16) for i in n]
            P = [_dot(Pb[i], Pb[i]) for i in n]
            yield
            T = [T[i] + _dot(T[i].astype(BF16), P[i].astype(BF16)) for i in n]
            yield
        vb2 = [t2(v[i]).astype(BF16) for i in n]
        X1 = [_dot(Bm[i], vb2[i]) for i in n]
        yield
        TU = [_dot(T[i].astype(BF16), jnp.concatenate([X1[i], t2(ea[i])], axis=1).astype(BF16)) for i in n]
        yield
        YQ = [_dot(arbr[i], jnp.concatenate(
            [TU[i].astype(BF16), jnp.concatenate([vb2[i], jnp.zeros_like(vb2[i])], axis=1)], axis=0)) for i in n]
        bk = [jnp.concatenate([beta[i] * e_end[i], kmod[i] * e_end[i]], axis=0).astype(BF16) for i in n]
        right = [jnp.concatenate([
            jnp.concatenate([sel(TU[i][:, LANES:]), sel(TU[i][:, :LANES])], axis=1),
            jnp.concatenate([jnp.zeros_like(v[i]), v[i]], axis=1)], axis=0).astype(BF16) for i in n]
        GH = [_dot_tn(bk[i], right[i]) for i in n]
        yield
        for i in n:
            _, d, _, pos = insts[i]
            qrows = pl.ds(pl.multiple_of(pos * C, C), C)
            Y1[d, qrows, :] = sel(YQ[i][:, :LANES])
            Q[d, qrows, :] = rb[i] + sel(YQ[i][:, LANES:])
            G[d, pos] = jnp.where(eye, jnp.exp(c_end[i]), 0.0) + jnp.where(same_head, GH[i][:, :LANES], 0.0)
            Hs[d, pos] = jnp.where(same_head, GH[i][:, LANES:], 0.0)

    def phase2(positions):
        z = [z_scr[0], z_scr[1]]
        for pos in positions:
            for d in range(2):
                qrows = pl.ds(pl.multiple_of(pos * C, C), C)
                if d == 0:
                    dest = pos * C
                else:
                    dest = jnp.where(pos < nc, (nc - 1 - pos) * C, Lc + (nc + nl - 1 - pos) * C)
                Yo[d, pl.ds(pl.multiple_of(dest, C), C), :] = _dot(Q[d, qrows, :].astype(BF16), z[d].astype(BF16)) + Y1[d, qrows, :]
                g_hi, g_lo = _split(G[d, pos])
                z_hi, z_lo = _split(z[d])
                z[d] = _dot(g_hi, z_hi) + _dot(g_hi, z_lo) + _dot(g_lo, z_hi) + Hs[d, pos]
            yield
        z_scr[0] = z[0]
        z_scr[1] = z[1]

    def run(items, positions):
        g2 = phase2(positions) if positions else iter(())
        if items:
            for stage, _ in enumerate(phase1(items)):
                if stage % 3 == 1:
                    next(g2, None)
        for _ in g2:
            pass

    lat_refs = (rl, kl, vl, ldl, al)
    ctx_refs = (rc, kc, vc, ldc, ac)
    U = 4
    assert nc == U and nl % U == 0

    z_scr[...] = jnp.zeros_like(z_scr)
    items = []
    for u in range(U):
        items += [(ctx_refs, 0, u * C, u), (ctx_refs, 1, (nc - 1 - u) * C, u)]
    run(items, None)

    def body(t, carry):
        items = []
        for u in range(U):
            items += [(lat_refs, 0, pl.multiple_of((t * U + u) * C, C), nc + t * U + u),
                      (lat_refs, 1, pl.multiple_of((nl - 1 - t * U - u) * C, C), nc + t * U + u)]
        run(items, [t * U + u for u in range(U)])
        return carry

    lax.fori_loop(0, nl // U, body, 0)
    run(None, [nl + u for u in range(U)])

    inv_n = 1.0 / RW_HEAD
    RB = 256

    def readout(refs, g_ref, y0, o_ref):
        r_ref, k_ref, v_ref, _, a_ref = refs

        def body(t, carry):
            rows = pl.ds(pl.multiple_of(t * RB, RB), RB)
            yrows = pl.ds(pl.multiple_of(y0 + t * RB, RB), RB)
            y = Yo[0, yrows, :] + Yo[1, yrows, :]
            mean = segsum(y) * inv_n
            yc = y - mean
            var = segsum(yc * yc) * inv_n
            yn = yc * lax.rsqrt(var + RW_LN_EPS) * lnw_ref[...] + lnb_ref[...]
            k = k_ref[rows, :]
            k_sum = k * (1.0 + (a_ref[0, rows, :] - 1.0) * k_a) + k * (1.0 + (a_ref[1, rows, :] - 1.0) * k_a)
            bonus = segsum(r_ref[rows, :] * k_sum * rk_ref[...]) * v_ref[rows, :]
            o_ref[rows, :] = ((yn + bonus) * g_ref[rows, :]).astype(BF16)
            return carry

        lax.fori_loop(0, r_ref.shape[0] // RB, body, 0)

    readout(ctx_refs, gc, 0, oc_ref)
    readout(lat_refs, gl, Lc, ol_ref)


def _rwkv_scan(r, k, v, g, ld, a, p, *, B, L, Lc):
    npair = D_MODEL // LANES
    ctx0 = B * L // Lc
    n_chunks = (L + Lc) // RW_CHUNK
    lat = pl.BlockSpec((L, LANES), lambda b, h: (b, h))
    ctx = pl.BlockSpec((Lc, LANES), lambda b, h: (ctx0 + b, h))
    lat2 = pl.BlockSpec((2, L, LANES), lambda b, h: (0, b, h))
    ctx2 = pl.BlockSpec((2, Lc, LANES), lambda b, h: (0, ctx0 + b, h))
    vec = pl.BlockSpec((1, LANES), lambda b, h: (0, h))
    vecs = [p["k_k"], p["k_a"], p["r_k"], p["ln_w"], p["ln_b"]]
    return pl.pallas_call(
        _rwkv_scan_kernel,
        grid=(B, npair),
        in_specs=[lat, lat, lat, lat, lat2, lat2, ctx, ctx, ctx, ctx, ctx2, ctx2] + [vec] * 5,
        out_specs=[
            pl.BlockSpec((L, LANES), lambda b, h: (b, h)),
            pl.BlockSpec((Lc, LANES), lambda b, h: (b, h)),
        ],
        out_shape=[
            jax.ShapeDtypeStruct((B * L, D_MODEL), BF16),
            jax.ShapeDtypeStruct((B * Lc, D_MODEL), BF16),
        ],
        scratch_shapes=[
            pltpu.VMEM((2, n_chunks, LANES, LANES), F32),
            pltpu.VMEM((2, n_chunks, LANES, LANES), F32),
            pltpu.VMEM((2, L + Lc, LANES), F32),
            pltpu.VMEM((2, L + Lc, LANES), F32),
            pltpu.VMEM((2, L + Lc, LANES), F32),
            pltpu.VMEM((2, LANES, LANES), F32),
        ],
        compiler_params=_params(2),
        name="rwkv_scan",
    )(r, k, v, g, ld, a, r, k, v, g, ld, a, *[x.reshape(1, D_MODEL) for x in vecs])


def _rope_rot(x, lane):
    n = x.shape[-1]
    w = (lane - MLA_NOPE) & 15
    return jnp.where(w < 8, -pltpu.roll(x, n - 8, 1), pltpu.roll(x, 8, 1))


def _mla_proj_kernel(x_ref, g_ref, mod_ref, wd_ref, qn_ref, kvn_ref, wuq_ref, wukv_ref, cos_ref, sin_ref,
                     q_ref, kv_ref, kr_ref):
    h = _norm_mod(x_ref[...], g_ref[...], mod_ref, 0, 1).astype(BF16)
    dq = _dot(h, wd_ref[...])
    cq = dq[:, :MLA_Q_LORA]
    ckv = dq[:, MLA_Q_LORA:MLA_Q_LORA + MLA_KV_LORA]
    kr = dq[:, MLA_Q_LORA + MLA_KV_LORA:]

    def rms(x, w):
        return x * lax.rsqrt(jnp.mean(x * x, axis=-1, keepdims=True) + NORM_EPS) * w

    q = _dot(rms(cq, qn_ref[...]).astype(BF16), wuq_ref[...])
    kv_ref[...] = _dot(rms(ckv, kvn_ref[...]).astype(BF16), wukv_ref[...]).astype(BF16)
    cos = cos_ref[...]
    sin = sin_ref[...]
    lane = lax.broadcasted_iota(jnp.int32, (1, LANES), 1)
    kr_ref[...] = (kr * cos + _rope_rot(kr, lane) * sin).astype(BF16)
    lane_q = lax.broadcasted_iota(jnp.int32, (1, q.shape[1]), 1) & (LANES - 1)
    reps = q.shape[1] // LANES
    cos_q = jnp.concatenate([cos] * reps, axis=1)
    sin_q = jnp.concatenate([sin] * reps, axis=1)
    q_ref[...] = ((q * cos_q + _rope_rot(q, lane_q) * sin_q) * (MLA_SCALE * LOG2E)).astype(BF16)


def _mla_proj(x, g, mod, p, cos_t, sin_t, *, L, Lc, B, tm=512):
    nt, d = x.shape
    n_lat_tiles = B * L // tm
    per_seq = L // tm
    nq = MLA_HEADS * LANES

    def full(a):
        nd = a.ndim
        return pl.BlockSpec(a.shape, lambda i: (0,) * nd)

    def tab(i):
        return (jnp.where(i < n_lat_tiles, i % per_seq, per_seq), 0)

    weights = [p["wd"], p["q_norm"], p["kv_norm"], p["wuq"], p["wukv"]]
    return pl.pallas_call(
        _mla_proj_kernel,
        grid=(nt // tm,),
        in_specs=[
            pl.BlockSpec((tm, d), lambda i: (i, 0)),
            pl.BlockSpec((1, d), lambda i: (0, 0)),
            pl.BlockSpec((1, 6, d), _mod_index(tm, L, B)),
        ] + [full(w) for w in weights] + [pl.BlockSpec((tm, LANES), tab), pl.BlockSpec((tm, LANES), tab)],
        out_specs=[
            pl.BlockSpec((tm, nq), lambda i: (i, 0)),
            pl.BlockSpec((tm, nq), lambda i: (i, 0)),
            pl.BlockSpec((tm, LANES), lambda i: (i, 0)),
        ],
        out_shape=[
            jax.ShapeDtypeStruct((nt, nq), BF16),
            jax.ShapeDtypeStruct((nt, nq), BF16),
            jax.ShapeDtypeStruct((nt, LANES), BF16),
        ],
        compiler_params=_params(1),
        name="mla_proj",
    )(x, g.reshape(1, d), mod, *weights, cos_t, sin_t)


def _attn_kernel(*refs, has_lat):
    if has_lat:
        q_ref, kvc_ref, krc_ref, kvl_ref, krl_ref, o_ref = refs
    else:
        q_ref, kvc_ref, krc_ref, o_ref = refs
    lane = lax.broadcasted_iota(jnp.int32, (1, LANES), 1)
    m0 = lane < MLA_NOPE
    heads = range(2)
    cols = [slice(hh * LANES, (hh + 1) * LANES) for hh in heads]
    q = [q_ref[:, cols[hh]] for hh in heads]
    kvs = [[kvc_ref[:, cols[hh]] for hh in heads]]
    krs = [krc_ref[...]]
    if has_lat:
        kvs.append([kvl_ref[:, cols[hh]] for hh in heads])
        krs.append(krl_ref[...])
    st = [[_dot_nt(jnp.where(m0, kv[hh], kr), q[hh]) for kv, kr in zip(kvs, krs)] for hh in heads]
    outs = []
    for hh in heads:
        m = st[hh][0].max(axis=0, keepdims=True)
        for s_ in st[hh][1:]:
            m = jnp.maximum(m, s_.max(axis=0, keepdims=True))
        den = None
        acc = None
        for part, s_ in enumerate(st[hh]):
            p = jnp.exp2(s_ - m)
            ps = p.sum(axis=0, keepdims=True)
            den = ps if den is None else den + ps
            o = _dot_tn(kvs[part][hh], p.astype(BF16))
            acc = o if acc is None else acc + o
        outs.append((acc / den)[MLA_NOPE:, :])
    o_ref[...] = jnp.concatenate(outs, axis=0).T.astype(BF16)


def _attention(q, kv, kr, *, B, L, Lc, latent, tq=512):
    npair = MLA_HEADS // 2
    ctx0 = B * L // Lc
    w2 = 2 * LANES
    kv_ctx = pl.BlockSpec((Lc, w2), lambda b, h, t: (ctx0 + b, h))
    kr_ctx = pl.BlockSpec((Lc, LANES), lambda b, h, t: (ctx0 + b, 0))
    if latent:
        nq_t = L // tq
        in_specs = [
            pl.BlockSpec((tq, w2), lambda b, h, t: (b * nq_t + t, h)),
            kv_ctx, kr_ctx,
            pl.BlockSpec((L, w2), lambda b, h, t: (b, h)),
            pl.BlockSpec((L, LANES), lambda b, h, t: (b, 0)),
        ]
        args = (q, kv, kr, kv, kr)
        rows = B * L
        out_spec = pl.BlockSpec((tq, LANES), lambda b, h, t: (b * nq_t + t, h))
    else:
        nq_t = 1
        in_specs = [pl.BlockSpec((Lc, w2), lambda b, h, t: (ctx0 + b, h)), kv_ctx, kr_ctx]
        args = (q, kv, kr)
        rows = B * Lc
        out_spec = pl.BlockSpec((Lc, LANES), lambda b, h, t: (b, h))
    return pl.pallas_call(
        functools.partial(_attn_kernel, has_lat=latent),
        grid=(B, npair, nq_t),
        in_specs=in_specs,
        out_specs=out_spec,
        out_shape=jax.ShapeDtypeStruct((rows, npair * LANES), BF16),
        compiler_params=_params(3),
        name="mla_attention_lat" if latent else "mla_attention_ctx",
    )(*args)


def _rope_tables(L, tm):
    n_rows = L // GRID_W
    row = jnp.repeat(jnp.arange(n_rows, dtype=F32), GRID_W)
    col = jnp.tile(jnp.arange(GRID_W, dtype=F32), n_rows)
    nq = MLA_ROPE // 4
    inv_freq = ROPE_BASE ** (-jnp.arange(nq, dtype=F32) / nq)
    ang_r = row[:, None] * inv_freq
    ang_c = col[:, None] * inv_freq
    ang = jnp.concatenate([ang_r, ang_r, ang_c, ang_c], axis=-1)
    cos = jnp.ones((L + tm, LANES), F32).at[:L, MLA_NOPE:MLA_NOPE + MLA_ROPE].set(jnp.cos(ang))
    sin = jnp.zeros((L + tm, LANES), F32).at[:L, MLA_NOPE:MLA_NOPE + MLA_ROPE].set(jnp.sin(ang))
    return cos, sin


def kernel(x, c, ctx, c_ctx, w_mod, b_mod, norm1, norm2, ffn_w_in, ffn_conv, ffn_conv_b, ffn_w_out, hg_w_in, hg_lb, hg_norm, hg_w_o, rw_mu, rw_w_rkv, rw_w0, rw_w1, rw_w2, rw_a0, rw_a1, rw_a2, rw_g1, rw_g2, rw_k_k, rw_k_a, rw_r_k, rw_ln_w, rw_ln_b, rw_w_o, mla_w_dqkv, mla_q_norm, mla_kv_norm, mla_w_uq, mla_w_ukv, mla_w_o, norm_f):
    B, L, D = x.shape
    Lc = ctx.shape[1]
    depth = w_mod.shape[0]
    assert D == D_MODEL and L & (L - 1) == 0 and Lc & (Lc - 1) == 0 and L % 2048 == 0 and (B * Lc) % 2048 == 0
    n_lat = B * L
    nt = n_lat + B * Lc

    X = jnp.concatenate([x.reshape(n_lat, D), ctx.reshape(B * Lc, D)], axis=0)

    mod_rows = 8 * ((B + 1 + 7) // 8)
    cc = jnp.zeros((mod_rows, D), F32).at[:B].set(c).at[B].set(c_ctx)
    mod_all = _modulation(cc, w_mod, b_mod)[:, :B + 1].reshape(depth, B + 1, 6, D)

    lb_p = jnp.cumsum(jax.nn.softmax(hg_lb.astype(F32), axis=1), axis=1)
    lower_bounds = lb_p - lb_p[:, :1]

    for layer in range(depth):
        last = layer == depth - 1
        kind, j = layer % N_MIXERS, layer // N_MIXERS
        mod = mod_all[layer]
        n_rows = n_lat if last else nt
        kw = dict(L=L, B=B)
        if kind == 0:
            u16, u32 = _norm_mod_matmul(X, norm1[layer], mod, hg_w_in[j].astype(BF16), n_bf16=3 * D, shift_i=0, scale_i=1, **kw)
            y_l, y_c = _hgrn_scan(u16, u32, lower_bounds[:, j], hg_norm[j], B=B, L=L, Lc=Lc)
            w_o = hg_w_o[j]
        elif kind == 1:
            p = dict(mu=rw_mu[j], w_rkv=rw_w_rkv[j].astype(BF16), w0=rw_w0[j], w1=rw_w1[j].astype(BF16),
                     w2=rw_w2[j].astype(BF16), a0=rw_a0[j], a1=rw_a1[j].astype(BF16), a2=rw_a2[j].astype(BF16),
                     g1=rw_g1[j].astype(BF16), g2=rw_g2[j].astype(BF16),
                     k_k=rw_k_k[j], k_a=rw_k_a[j], r_k=rw_r_k[j].reshape(D), ln_w=rw_ln_w[j], ln_b=rw_ln_b[j])
            r, k, v, g, ld, a = _rwkv_proj(X, norm1[layer], mod, p, L=L, Lc=Lc, B=B)
            y_l, y_c = _rwkv_scan(r, k, v, g, ld, a, p, B=B, L=L, Lc=Lc)
            w_o = rw_w_o[j]
        else:
            nlq = MLA_Q_LORA + MLA_KV_LORA
            wd = jnp.zeros((D, nlq + LANES), F32).at[:, :nlq].set(mla_w_dqkv[j][:, :nlq])
            wd = wd.at[:, nlq + MLA_NOPE:nlq + MLA_NOPE + MLA_ROPE].set(mla_w_dqkv[j][:, nlq:])
            wuq = mla_w_uq[j].reshape(MLA_Q_LORA, MLA_HEADS, MLA_NOPE + MLA_ROPE)
            wuq = jnp.pad(wuq, ((0, 0), (0, 0), (0, LANES - MLA_NOPE - MLA_ROPE))).reshape(MLA_Q_LORA, MLA_HEADS * LANES)
            p = dict(wd=wd.astype(BF16), q_norm=mla_q_norm[j].reshape(1, -1), kv_norm=mla_kv_norm[j].reshape(1, -1),
                     wuq=wuq.astype(BF16), wukv=mla_w_ukv[j].astype(BF16))
            tm_p = 512
            cos_t, sin_t = _rope_tables(L, tm_p)
            q, kv, kr = _mla_proj(X, norm1[layer], mod, p, cos_t, sin_t, L=L, Lc=Lc, B=B, tm=tm_p)
            y_l = _attention(q, kv, kr, B=B, L=L, Lc=Lc, latent=True)
            y_c = _attention(q, kv, kr, B=B, L=L, Lc=Lc, latent=False)
            w_o = mla_w_o[j]
        y = y_l if last else (y_l, y_c)
        X = _matmul_residual(y, w_o.astype(BF16), X, mod, gate_i=2, n_rows=n_rows, **kw)
        gmid = _ffn_gate(X, norm2[layer], mod, ffn_w_in[layer].astype(BF16), ffn_conv[layer], ffn_conv_b[layer],
                         L=L, Lc=Lc, B=B, n_rows=n_rows)
        X = _matmul_residual(gmid, ffn_w_out[layer].astype(BF16), X, mod, gate_i=5, n_rows=n_rows,
                             norm_f=norm_f if last else None, **kw)
    return X.reshape(B, L, D)
```

```python
import functools

import jax
import jax.numpy as jnp
from jax import lax
from jax.experimental import pallas as pl
from jax.experimental.pallas import tpu as pltpu

F32 = jnp.float32
BF16 = jnp.bfloat16

D_MODEL = 1024
N_MIXERS = 3
NORM_EPS = 1e-6
HG_HEADS = 8
HG_CHUNK = 64
RW_HEAD = 64
RW_LN_EPS = 64e-5
RW_CHUNK = 64
MLA_HEADS = 16
MLA_NOPE = 64
MLA_ROPE = 32
MLA_V = 64
MLA_Q_LORA = 256
MLA_KV_LORA = 256
MLA_SCALE = (MLA_NOPE + MLA_ROPE) ** -0.5
ROPE_BASE = 10000.0
LOG2E = 1.4426950408889634
GRID_W = 64
HALO = 16
LANES = 128
VMEM_LIMIT = 56 * 1024 * 1024


def _params(n_axes):
    return pltpu.CompilerParams(dimension_semantics=("arbitrary",) * n_axes, vmem_limit_bytes=VMEM_LIMIT)


def _dot(a, b):
    return jnp.dot(a, b, preferred_element_type=F32)


def _dot_nt(a, b):
    return lax.dot_general(a, b, (((1,), (1,)), ((), ())), preferred_element_type=F32)


def _dot_tn(a, b):
    return lax.dot_general(a, b, (((0,), (0,)), ((), ())), preferred_element_type=F32)


def _split(x):
    hi = x.astype(BF16)
    lo = (x - hi.astype(F32)).astype(BF16)
    return hi, lo


def _dot_exact_lhs(m_bf16, x):
    hi, lo = _split(x)
    return _dot(m_bf16, hi) + _dot(m_bf16, lo)


def _dot_exact_rhs(x, m_bf16):
    hi, lo = _split(x)
    return _dot(hi, m_bf16) + _dot(lo, m_bf16)


def _silu(x):
    return x * jax.nn.sigmoid(x)


def _norm_mod(x, g, mod_ref, shift_i, scale_i):
    ms = jnp.mean(x * x, axis=-1, keepdims=True)
    y = x * lax.rsqrt(ms + NORM_EPS) * g
    return y * (1.0 + mod_ref[0, scale_i:scale_i + 1, :]) + mod_ref[0, shift_i:shift_i + 1, :]


def _mod_index(tm, L, B):
    return lambda i, *_: (jnp.minimum(i * tm // L, B), 0, 0)


def _mod_kernel(c_ref, w_ref, b_ref, o_ref):
    s = _silu(c_ref[...]).astype(BF16)
    o_ref[0] = _dot(s, w_ref[0].astype(BF16)) + b_ref[0]


def _modulation(cc, w_mod, b_mod):
    depth, d, n = w_mod.shape
    rows = cc.shape[0]
    tn = 1024
    return pl.pallas_call(
        _mod_kernel,
        grid=(depth, n // tn),
        in_specs=[
            pl.BlockSpec((rows, d), lambda l, j: (0, 0)),
            pl.BlockSpec((1, d, tn), lambda l, j: (l, 0, j)),
            pl.BlockSpec((1, 1, tn), lambda l, j: (l, 0, j)),
        ],
        out_specs=pl.BlockSpec((1, rows, tn), lambda l, j: (l, 0, j)),
        out_shape=jax.ShapeDtypeStruct((depth, rows, n), F32),
        compiler_params=_params(2),
        name="modulation",
    )(cc, w_mod, b_mod.reshape(depth, 1, n))


def _nmm_kernel(x_ref, g_ref, mod_ref, w_ref, o16_ref, o32_ref, *, shift_i, scale_i, tn, n16):
    h = _norm_mod(x_ref[...], g_ref[...], mod_ref, shift_i, scale_i).astype(BF16)
    spt = tn // LANES
    for j in range(w_ref.shape[1] // tn):
        acc = _dot(h, w_ref[:, j * tn:(j + 1) * tn])
        o_ref, base = (o16_ref, j * spt) if j < n16 else (o32_ref, (j - n16) * spt)
        for s in range(spt):
            o_ref[base + s] = acc[:, s * LANES:(s + 1) * LANES].astype(o_ref.dtype)


def _norm_mod_matmul(x, g, mod, w, *, n_bf16, shift_i, scale_i, L, B, tm=512, tn=512):
    nt, d = x.shape
    n = w.shape[1]
    n16 = n_bf16 // tn
    assert n_bf16 % tn == 0 and n % tn == 0
    s16, s32 = n_bf16 // LANES, (n - n_bf16) // LANES
    return pl.pallas_call(
        functools.partial(_nmm_kernel, shift_i=shift_i, scale_i=scale_i, tn=tn, n16=n16),
        grid=(nt // tm,),
        in_specs=[
            pl.BlockSpec((tm, d), lambda i: (i, 0)),
            pl.BlockSpec((1, d), lambda i: (0, 0)),
            pl.BlockSpec((1, 6, d), _mod_index(tm, L, B)),
            pl.BlockSpec((d, n), lambda i: (0, 0)),
        ],
        out_specs=[
            pl.BlockSpec((s16, tm, LANES), lambda i: (0, i, 0)),
            pl.BlockSpec((s32, tm, LANES), lambda i: (0, i, 0)),
        ],
        out_shape=[
            jax.ShapeDtypeStruct((s16, nt, LANES), BF16),
            jax.ShapeDtypeStruct((s32, nt, LANES), F32),
        ],
        compiler_params=_params(1),
        name="norm_mod_matmul",
    )(x, g.reshape(1, d), mod, w)


def _mmres_kernel(*refs, gate_i, final, n_lat_tiles):
    if n_lat_tiles is None:
        a_ref, w_ref, x_ref, mod_ref, *rest = refs
        if len(a_ref.shape) == 3:
            a = jnp.concatenate([a_ref[c] for c in range(a_ref.shape[0])], axis=1)
        else:
            a = a_ref[...]
    else:
        al_ref, ac_ref, w_ref, x_ref, mod_ref, *rest = refs
        a = jnp.where(pl.program_id(0) < n_lat_tiles, al_ref[...], ac_ref[...])
    y = x_ref[...] + mod_ref[0, gate_i:gate_i + 1, :] * _dot(a, w_ref[...])
    if final:
        nf_ref, o_ref = rest
        ms = jnp.mean(y * y, axis=-1, keepdims=True)
        y = y * lax.rsqrt(ms + NORM_EPS) * nf_ref[...]
    else:
        (o_ref,) = rest
    o_ref[...] = y


def _matmul_residual(a, w, x, mod, *, gate_i, L, B, n_rows, tm=1024, norm_f=None):
    split = isinstance(a, tuple)
    blocked = not split and a.ndim == 3
    k, d = w.shape
    final = norm_f is not None
    if split:
        n_lat_tiles = a[0].shape[0] // tm
        a_specs = [
            pl.BlockSpec((tm, k), lambda i: (jnp.minimum(i, n_lat_tiles - 1), 0)),
            pl.BlockSpec((tm, k), lambda i: (jnp.maximum(i - n_lat_tiles, 0), 0)),
        ]
        a_args = list(a)
    else:
        n_lat_tiles = None
        if blocked:
            a_specs = [pl.BlockSpec((a.shape[0], tm, a.shape[2]), lambda i: (0, i, 0))]
        else:
            a_specs = [pl.BlockSpec((tm, k), lambda i: (i, 0))]
        a_args = [a]
    in_specs = a_specs + [
        pl.BlockSpec((k, d), lambda i: (0, 0)),
        pl.BlockSpec((tm, d), lambda i: (i, 0)),
        pl.BlockSpec((1, 6, d), _mod_index(tm, L, B)),
    ]
    args = a_args + [w, x, mod]
    if final:
        in_specs.append(pl.BlockSpec((1, d), lambda i: (0, 0)))
        args.append(norm_f.reshape(1, d))
    return pl.pallas_call(
        functools.partial(_mmres_kernel, gate_i=gate_i, final=final, n_lat_tiles=n_lat_tiles),
        grid=(n_rows // tm,),
        in_specs=in_specs,
        out_specs=pl.BlockSpec((tm, d), lambda i: (i, 0)),
        out_shape=jax.ShapeDtypeStruct((n_rows, d), F32),
        compiler_params=_params(1),
        name="matmul_residual",
    )(*args)


def _ffn1_kernel(x_ref, xp_ref, xn_ref, g_ref, mod_ref, w_ref, cw_ref, cb_ref, o_ref, h_scr,
                 *, tm, rb, nf, n_lat_tiles, L, Lc):
    i = pl.program_id(0)
    g = g_ref[...]
    h_scr[0:HALO, :] = _norm_mod(xp_ref[...], g, mod_ref, 3, 4).astype(BF16)
    h_scr[HALO:HALO + tm, :] = _norm_mod(x_ref[...], g, mod_ref, 3, 4).astype(BF16)
    h_scr[HALO + tm:, :] = _norm_mod(xn_ref[...], g, mod_ref, 3, 4).astype(BF16)
    is_lat = i < n_lat_tiles
    sub = lax.broadcasted_iota(jnp.int32, (8, 1), 0)
    nb = tm // rb

    def seq_edge(r0):
        return jnp.where(is_lat, ((i * tm + r0) % L) == 0, r0 % Lc == 0)

    def column_block(j, carry):
        a_blk, v_blk = [], []
        cw = cw_ref[j]
        cb = cb_ref[j]
        edge_rows = {}

        def gate(r):
            a = a_blk[r]
            up = jnp.where(seq_edge(r * rb), 0.0, a_blk[r - 1][rb - 1:rb, :] if r > 0 else edge_rows["before"])
            dn = jnp.where(seq_edge((r + 1) * rb), 0.0, a_blk[r + 1][0:1, :] if r < nb - 1 else edge_rows["after"])
            prev = pltpu.roll(a, 1, 0)
            prev = jnp.concatenate([jnp.where(sub == 0, up, prev[0:8, :]), prev[8:, :]], axis=0)
            nxt = pltpu.roll(a, rb - 1, 0)
            nxt = jnp.concatenate([nxt[:rb - 8, :], jnp.where(sub == 7, dn, nxt[rb - 8:, :])], axis=0)
            ac = cw[0:1, :] * prev + cw[1:2, :] * a + cw[2:3, :] * nxt + cb
            o_ref[j, r * rb:(r + 1) * rb, :] = (_silu(ac) * v_blk[r]).astype(BF16)

        for r in range(nb):
            lo = HALO + r * rb - (HALO if r == 0 else 0)
            hi = HALO + (r + 1) * rb + (HALO if r == nb - 1 else 0)
            a = _dot(h_scr[lo:hi, :], w_ref[j])
            if r == 0:
                edge_rows["before"], a = a[HALO - 1:HALO, :], a[HALO:, :]
            if r == nb - 1:
                edge_rows["after"], a = a[rb:rb + 1, :], a[:rb, :]
            a_blk.append(a)
            v_blk.append(_dot(h_scr[HALO + r * rb:HALO + (r + 1) * rb, :], w_ref[j + nf]))
            if r > 0:
                gate(r - 1)
        gate(nb - 1)
        return carry

    lax.fori_loop(0, nf, column_block, 0)


def _ffn_gate(x, g, mod, w_in, conv_w, conv_b, *, L, Lc, B, n_rows, tm=1024, tf=256, rb=256):
    assert L % rb == 0 and Lc % rb == 0 and tm % rb == 0
    assert tm % Lc == 0 and (L % tm == 0 or tm % L == 0)
    d = x.shape[1]
    dff = conv_w.shape[1]
    nf = dff // tf
    w_blk = w_in.reshape(d, 2 * nf, tf).transpose(1, 0, 2)
    cw_blk = conv_w.reshape(3, nf, tf).transpose(1, 0, 2)
    cb_blk = conv_b.reshape(nf, 1, tf)
    blkh = tm // HALO
    return pl.pallas_call(
        functools.partial(_ffn1_kernel, tm=tm, rb=rb, nf=nf, n_lat_tiles=B * L // tm, L=L, Lc=Lc),
        grid=(n_rows // tm,),
        in_specs=[
            pl.BlockSpec((tm, d), lambda i: (i, 0)),
            pl.BlockSpec((HALO, d), lambda i: (jnp.maximum(i * blkh - 1, 0), 0)),
            pl.BlockSpec((HALO, d), lambda i: (jnp.minimum((i + 1) * blkh, n_rows // HALO - 1), 0)),
            pl.BlockSpec((1, d), lambda i: (0, 0)),
            pl.BlockSpec((1, 6, d), _mod_index(tm, L, B)),
            pl.BlockSpec((2 * nf, d, tf), lambda i: (0, 0, 0)),
            pl.BlockSpec((nf, 3, tf), lambda i: (0, 0, 0)),
            pl.BlockSpec((nf, 1, tf), lambda i: (0, 0, 0)),
        ],
        out_specs=pl.BlockSpec((nf, tm, tf), lambda i: (0, i, 0)),
        out_shape=jax.ShapeDtypeStruct((nf, n_rows, tf), BF16),
        scratch_shapes=[pltpu.VMEM((tm + 2 * HALO, d), BF16)],
        compiler_params=_params(1),
        name="ffn_gate",
    )(x, x, x, g.reshape(1, d), mod, w_blk, cw_blk, cb_blk)


def _tri(n, rev):
    t = lax.broadcasted_iota(jnp.int32, (n, n), 0)
    s = lax.broadcasted_iota(jnp.int32, (n, n), 1)
    return (s >= t) if rev else (s <= t)


def _gla_group(q, k, v, lf, rev):
    n = len(q)
    c = q[0].shape[0]
    mid = c // 2
    tri = {r: _tri(c, r) for r in set(rev)}
    tri_b = {r: m.astype(BF16) for r, m in tri.items()}
    mid_row = [(c - 1 - mid) if rev[i] else mid for i in range(n)]
    end_row = [0 if rev[i] else c - 1 for i in range(n)]
    b = [_dot_exact_lhs(tri_b[rev[i]], lf[i]) for i in range(n)]
    b_mid = [b[i][mid_row[i]:mid_row[i] + 1, :] for i in range(n)]
    b_end = [b[i][end_row[i]:end_row[i] + 1, :] for i in range(n)]
    qs = [(q[i] * jnp.exp(b[i] - b_mid[i])).astype(BF16) for i in range(n)]
    ks = [(k[i] * jnp.exp(b_mid[i] - b[i])).astype(BF16) for i in range(n)]
    att = [_dot_nt(qs[i], ks[i]) for i in range(n)]
    att = [jnp.where(tri[rev[i]], att[i], 0.0).astype(BF16) for i in range(n)]
    vb = [v[i].astype(BF16) for i in range(n)]
    o_intra = [_dot(att[i], vb[i]) for i in range(n)]
    kd = [(k[i] * jnp.exp(b_end[i] - b[i])).astype(BF16) for i in range(n)]
    ds = [_dot_tn(vb[i], kd[i]) for i in range(n)]
    qd = [(q[i] * jnp.exp(b[i])).astype(BF16) for i in range(n)]
    e_end = [jnp.exp(b_end[i]) for i in range(n)]
    return o_intra, qd, ds, e_end


def _hgrn_kernel(ql, il, gl, zfl, zbl, qc, ic, gc, zfc, zbc, lb_ref, gn_ref, ol_ref, oc_ref,
                 of_l, ob_l, of_c, ob_c, sf, sb):
    C = HG_CHUNK
    lbf = lb_ref[0, 0]
    lbb = lb_ref[1, 0]

    def gates(z, lb):
        return jnp.log(lb + (1.0 - lb) * jax.nn.sigmoid(z)), (1.0 - lb) * jax.nn.sigmoid(-z)

    def part(q_ref, i_ref, zf_ref, zb_ref, of_ref, ob_ref, U):
        n = q_ref.shape[0] // C

        def body(t, carry):
            rows = [pl.ds(pl.multiple_of((t * U + u) * C, C), C) for u in range(U)]
            rows += [pl.ds(pl.multiple_of((n - 1 - t * U - u) * C, C), C) for u in range(U)]
            rev = [False] * U + [True] * U
            zs = [(zb_ref if r else zf_ref)[rw, :] for rw, r in zip(rows, rev)]
            gk = [gates(z, lbb if r else lbf) for z, r in zip(zs, rev)]
            q = [_silu(q_ref[rw, :].astype(F32)) for rw in rows]
            v = [i_ref[rw, :] for rw in rows]
            o_intra, qd, ds, e_end = _gla_group(q, [g[1] for g in gk], v, [g[0] for g in gk], rev)
            st = [sf[...], sb[...]]
            for u in range(U):
                for d in range(2):
                    i = d * U + u
                    o = o_intra[i] + _dot_nt(qd[i], st[d].astype(BF16))
                    (ob_ref if d else of_ref)[rows[i], :] = o
                    st[d] = st[d] * e_end[i] + ds[i]
            sf[...] = st[0]
            sb[...] = st[1]
            return carry

        lax.fori_loop(0, n // U, body, 0)

    sf[...] = jnp.zeros_like(sf)
    sb[...] = jnp.zeros_like(sb)
    part(qc, ic, zfc, zbc, of_c, ob_c, 4)
    part(ql, il, zfl, zbl, of_l, ob_l, 8)

    def readout(of_ref, ob_ref, g_ref, o_ref):
        o = of_ref[...] + ob_ref[...]
        ms = jnp.mean(o * o, axis=-1, keepdims=True)
        y = o * lax.rsqrt(ms + NORM_EPS) * gn_ref[...]
        o_ref[...] = (y * _silu(g_ref[...].astype(F32))).astype(BF16)

    readout(of_c, ob_c, gc, oc_ref)
    readout(of_l, ob_l, gl, ol_ref)


def _hgrn_scan(u16, u32, lbs, g_norm, *, B, L, Lc):
    H = HG_HEADS
    ctx0 = B * L // Lc

    def lat(sec):
        return pl.BlockSpec((None, L, LANES), lambda b, h: (sec * H + h, b, 0))

    def ctx(sec):
        return pl.BlockSpec((None, Lc, LANES), lambda b, h: (sec * H + h, ctx0 + b, 0))

    secs = (0, 1, 2, 0, 1)
    arrays = [u16, u16, u16, u32, u32]
    return pl.pallas_call(
        _hgrn_kernel,
        grid=(B, H),
        in_specs=[lat(s) for s in secs] + [ctx(s) for s in secs] + [
            pl.BlockSpec((2, 1, 1, LANES), lambda b, h: (0, h, 0, 0)),
            pl.BlockSpec((1, LANES), lambda b, h: (0, 0)),
        ],
        out_specs=[
            pl.BlockSpec((L, LANES), lambda b, h: (b, h)),
            pl.BlockSpec((Lc, LANES), lambda b, h: (b, h)),
        ],
        out_shape=[
            jax.ShapeDtypeStruct((B * L, H * LANES), BF16),
            jax.ShapeDtypeStruct((B * Lc, H * LANES), BF16),
        ],
        scratch_shapes=[
            pltpu.VMEM((L, LANES), F32), pltpu.VMEM((L, LANES), F32),
            pltpu.VMEM((Lc, LANES), F32), pltpu.VMEM((Lc, LANES), F32),
            pltpu.VMEM((LANES, LANES), F32), pltpu.VMEM((LANES, LANES), F32),
        ],
        compiler_params=_params(2),
        name="hgrn_scan",
    )(*(arrays * 2), lbs.reshape(2, H, 1, LANES), g_norm.reshape(1, LANES))


def _softplus(x):
    return jnp.maximum(x, 0.0) + jnp.log1p(jnp.exp(-jnp.abs(x)))


def _rwkv_proj_kernel(x_ref, xp_ref, xn_ref, g_ref, mod_ref, mu_ref, wrkv_ref, w0_ref, w1_ref, w2_ref,
                      a0_ref, a1_ref, a2_ref, g1_ref, g2_ref,
                      r_ref, k_ref, v_ref, gg_ref, ld_ref, a_ref, *, tm, n_lat_tiles, L, Lc):
    i = pl.program_id(0)
    g = g_ref[...]
    h = _norm_mod(x_ref[...], g, mod_ref, 0, 1)
    hp = _norm_mod(xp_ref[...], g, mod_ref, 0, 1)[7:8, :]
    hn = _norm_mod(xn_ref[...], g, mod_ref, 0, 1)[0:1, :]
    seqlen = jnp.where(i < n_lat_tiles, L, Lc)
    row = lax.broadcasted_iota(jnp.int32, (tm, 1), 0)
    pos = (i * tm + row) & (seqlen - 1)
    down = jnp.where(row == 0, hp, pltpu.roll(h, 1, 0))
    down = jnp.where(pos == 0, 0.0, down)
    up = jnp.where(row == tm - 1, hn, pltpu.roll(h, tm - 1, 0))
    up = jnp.where(pos == seqlen - 1, 0.0, up)
    dx = 0.5 * (down + up) - h

    def mix(j):
        return (h + dx * mu_ref[j:j + 1, :]).astype(BF16)

    xw, xa, xg = mix(1), mix(4), mix(5)
    t_w = [_dot(xw, w1_ref[d]) for d in range(2)]
    t_a = [_dot(xa, a1_ref[d]) for d in range(2)]
    t_g = _dot(xg, g1_ref[...])
    wl = [_dot(jnp.tanh(t_w[d]).astype(BF16), w2_ref[d]) for d in range(2)]
    al = [_dot(t_a[d].astype(BF16), a2_ref[d]) for d in range(2)]
    gg_ref[...] = _dot(jax.nn.sigmoid(t_g).astype(BF16), g2_ref[...])
    r = _dot(mix(0), wrkv_ref[0])
    k = _dot(mix(2), wrkv_ref[1])
    v = _dot(mix(3), wrkv_ref[2])
    for d in range(2):
        w_log = -_softplus(-(w0_ref[d:d + 1, :] + wl[d])) - 0.5
        ld_ref[d] = -jnp.exp(w_log)
        a_ref[d] = jax.nn.sigmoid(a0_ref[d:d + 1, :] + al[d])
    r_ref[...] = r
    k_ref[...] = k
    v_ref[...] = v


def _rwkv_proj(x, g, mod, p, *, L, Lc, B, tm=256):
    nt, d = x.shape
    blk8 = tm // 8

    def full(a):
        nd = a.ndim
        return pl.BlockSpec(a.shape, lambda i: (0,) * nd)

    weights = [p["mu"], p["w_rkv"], p["w0"], p["w1"], p["w2"], p["a0"], p["a1"], p["a2"], p["g1"], p["g2"]]
    row_spec = pl.BlockSpec((tm, d), lambda i: (i, 0))
    dir_spec = pl.BlockSpec((2, tm, d), lambda i: (0, i, 0))
    return pl.pallas_call(
        functools.partial(_rwkv_proj_kernel, tm=tm, n_lat_tiles=B * L // tm, L=L, Lc=Lc),
        grid=(nt // tm,),
        in_specs=[
            row_spec,
            pl.BlockSpec((8, d), lambda i: (jnp.maximum(i * blk8 - 1, 0), 0)),
            pl.BlockSpec((8, d), lambda i: (jnp.minimum((i + 1) * blk8, nt // 8 - 1), 0)),
            pl.BlockSpec((1, d), lambda i: (0, 0)),
            pl.BlockSpec((1, 6, d), _mod_index(tm, L, B)),
        ] + [full(w) for w in weights],
        out_specs=[row_spec, row_spec, row_spec, row_spec, dir_spec, dir_spec],
        out_shape=[jax.ShapeDtypeStruct((nt, d), F32)] * 4 + [jax.ShapeDtypeStruct((2, nt, d), F32)] * 2,
        compiler_params=_params(1),
        name="rwkv_proj",
    )(x, x, x, g.reshape(1, d), mod, *weights)


def _rwkv_scan_kernel(rl, kl, vl, gl, ldl, al, rc, kc, vc, gc, ldc, ac,
                      kk_ref, ka_ref, rk_ref, lnw_ref, lnb_ref, ol_ref, oc_ref,
                      G, Hs, Q, Y1, Yo, z_scr):
    C = RW_CHUNK
    C2 = 2 * C
    Lc = rc.shape[0]
    L = rl.shape[0]
    nc, nl = Lc // C, L // C
    lane = lax.broadcasted_iota(jnp.int32, (1, LANES), 1)
    m0 = lane < RW_HEAD
    ri = lax.broadcasted_iota(jnp.int32, (LANES, LANES), 0)
    ci = lax.broadcasted_iota(jnp.int32, (LANES, LANES), 1)
    same_head = (ri < RW_HEAD) == (ci < RW_HEAD)
    eye = ri == ci
    ones_bd = same_head.astype(BF16)
    k_k = kk_ref[...]
    k_a = ka_ref[...]

    def segsum(x):
        return _dot_exact_rhs(x, ones_bd)

    def x2(x):
        return jnp.concatenate([jnp.where(m0, x, 0.0), jnp.where(m0, 0.0, x)], axis=0)

    def t2(x):
        return jnp.concatenate([x, x], axis=0)

    def sel(x):
        return jnp.where(m0, x[:C, :], x[C:, :])

    tt = lax.broadcasted_iota(jnp.int32, (C2, C2), 0)
    ss = lax.broadcasted_iota(jnp.int32, (C2, C2), 1)
    sh = (tt < C) == (ss < C)
    tl, sl = tt & (C - 1), ss & (C - 1)
    strict = {False: sh & (sl < tl), True: sh & (sl > tl)}
    incl = {False: sh & (sl <= tl), True: sh & (sl >= tl)}
    eye2 = jnp.where(tt == ss, 1.0, 0.0)
    tri_b = {r: _tri(C, r).astype(BF16) for r in (False, True)}

    def phase1(insts):
        n = range(len(insts))
        rev = [d == 1 for (_, d, _, _) in insts]
        rows = [pl.ds(r0, C) for (_, _, r0, _) in insts]
        r = [insts[i][0][0][rows[i], :] for i in n]
        k = [insts[i][0][1][rows[i], :] for i in n]
        v = [insts[i][0][2][rows[i], :] for i in n]
        lw = [insts[i][0][3][insts[i][1], rows[i], :] for i in n]
        a = [insts[i][0][4][insts[i][1], rows[i], :] for i in n]
        kk = [k[i] * k_k for i in n]
        n2 = [segsum(kk[i] * kk[i]) for i in n]
        yield
        kk = [kk[i] / jnp.maximum(jnp.sqrt(n2[i]), 1e-12) for i in n]
        kmod = [k[i] * (1.0 + (a[i] - 1.0) * k_a) for i in n]
        beta = [kk[i] * a[i] for i in n]
        c_in = [_dot_exact_lhs(tri_b[rev[i]], lw[i]) for i in n]
        yield
        c_end = [c_in[i][(0 if rev[i] else C - 1):(0 if rev[i] else C - 1) + 1, :] for i in n]
        e_neg = [jnp.exp(-c_in[i]) for i in n]
        e_end = [jnp.exp(c_end[i] - c_in[i]) for i in n]
        ea = [-kk[i] * jnp.exp(c_in[i] - lw[i]) for i in n]
        rb = [r[i] * jnp.exp(c_in[i]) for i in n]
        lhs = [jnp.concatenate([x2(ea[i]), x2(rb[i])], axis=0).astype(BF16) for i in n]
        rhs = [jnp.concatenate([t2(beta[i] * e_neg[i]), t2(kmod[i] * e_neg[i])], axis=0).astype(BF16) for i in n]
        M = [_dot_nt(lhs[i], rhs[i]) for i in n]
        yield
        A = [jnp.where(strict[rev[i]], M[i][:C2, :C2], 0.0) for i in n]
        Bm = [jnp.where(strict[rev[i]], M[i][:C2, C2:], 0.0).astype(BF16) for i in n]
        arbr = [jnp.concatenate([jnp.where(incl[rev[i]], M[i][C2:, :C2], 0.0),
                                 jnp.where(incl[rev[i]], M[i][C2:, C2:], 0.0)], axis=1).astype(BF16) for i in n]
        T = [eye2 + A[i] for i in n]
        P = A
        for _ in range(C.bit_length() - 2):
            Pb = [P[i].astype(BF16) for i in n]
            P = [_dot(Pb[i], Pb[i]) for i in n]
            yield
            T = [T[i] + _dot(T[i].astype(BF16), P[i].astype(BF16)) for i in n]
            yield
        vb2 = [t2(v[i]).astype(BF16) for i in n]
        X1 = [_dot(Bm[i], vb2[i]) for i in n]
        yield
        TU = [_dot(T[i].astype(BF16), jnp.concatenate([X1[i], t2(ea[i])], axis=1).astype(BF16)) for i in n]
        yield
        YQ = [_dot(arbr[i], jnp.concatenate(
            [TU[i].astype(BF16), jnp.concatenate([vb2[i], jnp.zeros_like(vb2[i])], axis=1)], axis=0)) for i in n]
        bk = [jnp.concatenate([beta[i] * e_end[i], kmod[i] * e_end[i]], axis=0).astype(BF16) for i in n]
        right = [jnp.concatenate([
            jnp.concatenate([sel(TU[i][:, LANES:]), sel(TU[i][:, :LANES])], axis=1),
            jnp.concatenate([jnp.zeros_like(v[i]), v[i]], axis=1)], axis=0).astype(BF16) for i in n]
        GH = [_dot_tn(bk[i], right[i]) for i in n]
        yield
        for i in n:
            _, d, _, pos = insts[i]
            qrows = pl.ds(pl.multiple_of(pos * C, C), C)
            Y1[d, qrows, :] = sel(YQ[i][:, :LANES])
            Q[d, qrows, :] = rb[i] + sel(YQ[i][:, LANES:])
            G[d, pos] = jnp.where(eye, jnp.exp(c_end[i]), 0.0) + jnp.where(same_head, GH[i][:, :LANES], 0.0)
            Hs[d, pos] = jnp.where(same_head, GH[i][:, LANES:], 0.0)

    def phase2(positions):
        z = [z_scr[0], z_scr[1]]
        for pos in positions:
            for d in range(2):
                qrows = pl.ds(pl.multiple_of(pos * C, C), C)
                if d == 0:
                    dest = pos * C
                else:
                    dest = jnp.where(pos < nc, (nc - 1 - pos) * C, Lc + (nc + nl - 1 - pos) * C)
                Yo[d, pl.ds(pl.multiple_of(dest, C), C), :] = _dot(Q[d, qrows, :].astype(BF16), z[d].astype(BF16)) + Y1[d, qrows, :]
                g_hi, g_lo = _split(G[d, pos])
                z_hi, z_lo = _split(z[d])
                z[d] = _dot(g_hi, z_hi) + _dot(g_hi, z_lo) + _dot(g_lo, z_hi) + Hs[d, pos]
            yield
        z_scr[0] = z[0]
        z_scr[1] = z[1]

    def run(items, positions):
        g2 = phase2(positions) if positions else iter(())
        if items:
            for stage, _ in enumerate(phase1(items)):
                if stage % 3 == 1:
                    next(g2, None)
        for _ in g2:
            pass

    lat_refs = (rl, kl, vl, ldl, al)
    ctx_refs = (rc, kc, vc, ldc, ac)
    U = 4
    assert nc == U and nl % U == 0

    z_scr[...] = jnp.zeros_like(z_scr)
    items = []
    for u in range(U):
        items += [(ctx_refs, 0, u * C, u), (ctx_refs, 1, (nc - 1 - u) * C, u)]
    run(items, None)

    def body(t, carry):
        items = []
        for u in range(U):
            items += [(lat_refs, 0, pl.multiple_of((t * U + u) * C, C), nc + t * U + u),
                      (lat_refs, 1, pl.multiple_of((nl - 1 - t * U - u) * C, C), nc + t * U + u)]
        run(items, [t * U + u for u in range(U)])
        return carry

    lax.fori_loop(0, nl // U, body, 0)
    run(None, [nl + u for u in range(U)])

    inv_n = 1.0 / RW_HEAD
    RB = 256

    def readout(refs, g_ref, y0, o_ref):
        r_ref, k_ref, v_ref, _, a_ref = refs

        def body(t, carry):
            rows = pl.ds(pl.multiple_of(t * RB, RB), RB)
            yrows = pl.ds(pl.multiple_of(y0 + t * RB, RB), RB)
            y = Yo[0, yrows, :] + Yo[1, yrows, :]
            mean = segsum(y) * inv_n
            yc = y - mean
            var = segsum(yc * yc) * inv_n
            yn = yc * lax.rsqrt(var + RW_LN_EPS) * lnw_ref[...] + lnb_ref[...]
            k = k_ref[rows, :]
            k_sum = k * (1.0 + (a_ref[0, rows, :] - 1.0) * k_a) + k * (1.0 + (a_ref[1, rows, :] - 1.0) * k_a)
            bonus = segsum(r_ref[rows, :] * k_sum * rk_ref[...]) * v_ref[rows, :]
            o_ref[rows, :] = ((yn + bonus) * g_ref[rows, :]).astype(BF16)
            return carry

        lax.fori_loop(0, r_ref.shape[0] // RB, body, 0)

    readout(ctx_refs, gc, 0, oc_ref)
    readout(lat_refs, gl, Lc, ol_ref)


def _rwkv_scan(r, k, v, g, ld, a, p, *, B, L, Lc):
    npair = D_MODEL // LANES
    ctx0 = B * L // Lc
    n_chunks = (L + Lc) // RW_CHUNK
    lat = pl.BlockSpec((L, LANES), lambda b, h: (b, h))
    ctx = pl.BlockSpec((Lc, LANES), lambda b, h: (ctx0 + b, h))
    lat2 = pl.BlockSpec((2, L, LANES), lambda b, h: (0, b, h))
    ctx2 = pl.BlockSpec((2, Lc, LANES), lambda b, h: (0, ctx0 + b, h))
    vec = pl.BlockSpec((1, LANES), lambda b, h: (0, h))
    vecs = [p["k_k"], p["k_a"], p["r_k"], p["ln_w"], p["ln_b"]]
    return pl.pallas_call(
        _rwkv_scan_kernel,
        grid=(B, npair),
        in_specs=[lat, lat, lat, lat, lat2, lat2, ctx, ctx, ctx, ctx, ctx2, ctx2] + [vec] * 5,
        out_specs=[
            pl.BlockSpec((L, LANES), lambda b, h: (b, h)),
            pl.BlockSpec((Lc, LANES), lambda b, h: (b, h)),
        ],
        out_shape=[
            jax.ShapeDtypeStruct((B * L, D_MODEL), BF16),
            jax.ShapeDtypeStruct((B * Lc, D_MODEL), BF16),
        ],
        scratch_shapes=[
            pltpu.VMEM((2, n_chunks, LANES, LANES), F32),
            pltpu.VMEM((2, n_chunks, LANES, LANES), F32),
            pltpu.VMEM((2, L + Lc, LANES), F32),
            pltpu.VMEM((2, L + Lc, LANES), F32),
            pltpu.VMEM((2, L + Lc, LANES), F32),
            pltpu.VMEM((2, LANES, LANES), F32),
        ],
        compiler_params=_params(2),
        name="rwkv_scan",
    )(r, k, v, g, ld, a, r, k, v, g, ld, a, *[x.reshape(1, D_MODEL) for x in vecs])


def _rope_rot(x, lane):
    n = x.shape[-1]
    w = (lane - MLA_NOPE) & 15
    return jnp.where(w < 8, -pltpu.roll(x, n - 8, 1), pltpu.roll(x, 8, 1))


def _mla_proj_kernel(x_ref, g_ref, mod_ref, wd_ref, qn_ref, kvn_ref, wuq_ref, wukv_ref, cos_ref, sin_ref,
                     q_ref, kv_ref, kr_ref):
    h = _norm_mod(x_ref[...], g_ref[...], mod_ref, 0, 1).astype(BF16)
    dq = _dot(h, wd_ref[...])
    cq = dq[:, :MLA_Q_LORA]
    ckv = dq[:, MLA_Q_LORA:MLA_Q_LORA + MLA_KV_LORA]
    kr = dq[:, MLA_Q_LORA + MLA_KV_LORA:]

    def rms(x, w):
        return x * lax.rsqrt(jnp.mean(x * x, axis=-1, keepdims=True) + NORM_EPS) * w

    q = _dot(rms(cq, qn_ref[...]).astype(BF16), wuq_ref[...])
    kv_ref[...] = _dot(rms(ckv, kvn_ref[...]).astype(BF16), wukv_ref[...]).astype(BF16)
    cos = cos_ref[...]
    sin = sin_ref[...]
    lane = lax.broadcasted_iota(jnp.int32, (1, LANES), 1)
    kr_ref[...] = (kr * cos + _rope_rot(kr, lane) * sin).astype(BF16)
    lane_q = lax.broadcasted_iota(jnp.int32, (1, q.shape[1]), 1) & (LANES - 1)
    reps = q.shape[1] // LANES
    cos_q = jnp.concatenate([cos] * reps, axis=1)
    sin_q = jnp.concatenate([sin] * reps, axis=1)
    q_ref[...] = ((q * cos_q + _rope_rot(q, lane_q) * sin_q) * (MLA_SCALE * LOG2E)).astype(BF16)


def _mla_proj(x, g, mod, p, cos_t, sin_t, *, L, Lc, B, tm=512):
    nt, d = x.shape
    n_lat_tiles = B * L // tm
    per_seq = L // tm
    nq = MLA_HEADS * LANES

    def full(a):
        nd = a.ndim
        return pl.BlockSpec(a.shape, lambda i: (0,) * nd)

    def tab(i):
        return (jnp.where(i < n_lat_tiles, i % per_seq, per_seq), 0)

    weights = [p["wd"], p["q_norm"], p["kv_norm"], p["wuq"], p["wukv"]]
    return pl.pallas_call(
        _mla_proj_kernel,
        grid=(nt // tm,),
        in_specs=[
            pl.BlockSpec((tm, d), lambda i: (i, 0)),
            pl.BlockSpec((1, d), lambda i: (0, 0)),
            pl.BlockSpec((1, 6, d), _mod_index(tm, L, B)),
        ] + [full(w) for w in weights] + [pl.BlockSpec((tm, LANES), tab), pl.BlockSpec((tm, LANES), tab)],
        out_specs=[
            pl.BlockSpec((tm, nq), lambda i: (i, 0)),
            pl.BlockSpec((tm, nq), lambda i: (i, 0)),
            pl.BlockSpec((tm, LANES), lambda i: (i, 0)),
        ],
        out_shape=[
            jax.ShapeDtypeStruct((nt, nq), BF16),
            jax.ShapeDtypeStruct((nt, nq), BF16),
            jax.ShapeDtypeStruct((nt, LANES), BF16),
        ],
        compiler_params=_params(1),
        name="mla_proj",
    )(x, g.reshape(1, d), mod, *weights, cos_t, sin_t)


def _attn_kernel(*refs, has_lat):
    if has_lat:
        q_ref, kvc_ref, krc_ref, kvl_ref, krl_ref, o_ref = refs
    else:
        q_ref, kvc_ref, krc_ref, o_ref = refs
    lane = lax.broadcasted_iota(jnp.int32, (1, LANES), 1)
    m0 = lane < MLA_NOPE
    heads = range(2)
    cols = [slice(hh * LANES, (hh + 1) * LANES) for hh in heads]
    q = [q_ref[:, cols[hh]] for hh in heads]
    kvs = [[kvc_ref[:, cols[hh]] for hh in heads]]
    krs = [krc_ref[...]]
    if has_lat:
        kvs.append([kvl_ref[:, cols[hh]] for hh in heads])
        krs.append(krl_ref[...])
    st = [[_dot_nt(jnp.where(m0, kv[hh], kr), q[hh]) for kv, kr in zip(kvs, krs)] for hh in heads]
    outs = []
    for hh in heads:
        m = st[hh][0].max(axis=0, keepdims=True)
        for s_ in st[hh][1:]:
            m = jnp.maximum(m, s_.max(axis=0, keepdims=True))
        den = None
        acc = None
        for part, s_ in enumerate(st[hh]):
            p = jnp.exp2(s_ - m)
            ps = p.sum(axis=0, keepdims=True)
            den = ps if den is None else den + ps
            o = _dot_tn(kvs[part][hh], p.astype(BF16))
            acc = o if acc is None else acc + o
        outs.append((acc / den)[MLA_NOPE:, :])
    o_ref[...] = jnp.concatenate(outs, axis=0).T.astype(BF16)


def _attention(q, kv, kr, *, B, L, Lc, latent, tq=512):
    npair = MLA_HEADS // 2
    ctx0 = B * L // Lc
    w2 = 2 * LANES
    kv_ctx = pl.BlockSpec((Lc, w2), lambda b, h, t: (ctx0 + b, h))
    kr_ctx = pl.BlockSpec((Lc, LANES), lambda b, h, t: (ctx0 + b, 0))
    if latent:
        nq_t = L // tq
        in_specs = [
            pl.BlockSpec((tq, w2), lambda b, h, t: (b * nq_t + t, h)),
            kv_ctx, kr_ctx,
            pl.BlockSpec((L, w2), lambda b, h, t: (b, h)),
            pl.BlockSpec((L, LANES), lambda b, h, t: (b, 0)),
        ]
        args = (q, kv, kr, kv, kr)
        rows = B * L
        out_spec = pl.BlockSpec((tq, LANES), lambda b, h, t: (b * nq_t + t, h))
    else:
        nq_t = 1
        in_specs = [pl.BlockSpec((Lc, w2), lambda b, h, t: (ctx0 + b, h)), kv_ctx, kr_ctx]
        args = (q, kv, kr)
        rows = B * Lc
        out_spec = pl.BlockSpec((Lc, LANES), lambda b, h, t: (b, h))
    return pl.pallas_call(
        functools.partial(_attn_kernel, has_lat=latent),
        grid=(B, npair, nq_t),
        in_specs=in_specs,
        out_specs=out_spec,
        out_shape=jax.ShapeDtypeStruct((rows, npair * LANES), BF16),
        compiler_params=_params(3),
        name="mla_attention_lat" if latent else "mla_attention_ctx",
    )(*args)


def _rope_tables(L, tm):
    n_rows = L // GRID_W
    row = jnp.repeat(jnp.arange(n_rows, dtype=F32), GRID_W)
    col = jnp.tile(jnp.arange(GRID_W, dtype=F32), n_rows)
    nq = MLA_ROPE // 4
    inv_freq = ROPE_BASE ** (-jnp.arange(nq, dtype=F32) / nq)
    ang_r = row[:, None] * inv_freq
    ang_c = col[:, None] * inv_freq
    ang = jnp.concatenate([ang_r, ang_r, ang_c, ang_c], axis=-1)
    cos = jnp.ones((L + tm, LANES), F32).at[:L, MLA_NOPE:MLA_NOPE + MLA_ROPE].set(jnp.cos(ang))
    sin = jnp.zeros((L + tm, LANES), F32).at[:L, MLA_NOPE:MLA_NOPE + MLA_ROPE].set(jnp.sin(ang))
    return cos, sin


def kernel(x, c, ctx, c_ctx, w_mod, b_mod, norm1, norm2, ffn_w_in, ffn_conv, ffn_conv_b, ffn_w_out, hg_w_in, hg_lb, hg_norm, hg_w_o, rw_mu, rw_w_rkv, rw_w0, rw_w1, rw_w2, rw_a0, rw_a1, rw_a2, rw_g1, rw_g2, rw_k_k, rw_k_a, rw_r_k, rw_ln_w, rw_ln_b, rw_w_o, mla_w_dqkv, mla_q_norm, mla_kv_norm, mla_w_uq, mla_w_ukv, mla_w_o, norm_f):
    B, L, D = x.shape
    Lc = ctx.shape[1]
    depth = w_mod.shape[0]
    assert D == D_MODEL and L & (L - 1) == 0 and Lc & (Lc - 1) == 0 and L % 2048 == 0 and (B * Lc) % 2048 == 0
    n_lat = B * L
    nt = n_lat + B * Lc

    X = jnp.concatenate([x.reshape(n_lat, D), ctx.reshape(B * Lc, D)], axis=0)

    mod_rows = 8 * ((B + 1 + 7) // 8)
    cc = jnp.zeros((mod_rows, D), F32).at[:B].set(c).at[B].set(c_ctx)
    mod_all = _modulation(cc, w_mod, b_mod)[:, :B + 1].reshape(depth, B + 1, 6, D)

    lb_p = jnp.cumsum(jax.nn.softmax(hg_lb.astype(F32), axis=1), axis=1)
    lower_bounds = lb_p - lb_p[:, :1]

    for layer in range(depth):
        last = layer == depth - 1
        kind, j = layer % N_MIXERS, layer // N_MIXERS
        mod = mod_all[layer]
        n_rows = n_lat if last else nt
        kw = dict(L=L, B=B)
        if kind == 0:
            u16, u32 = _norm_mod_matmul(X, norm1[layer], mod, hg_w_in[j].astype(BF16), n_bf16=3 * D, shift_i=0, scale_i=1, **kw)
            y_l, y_c = _hgrn_scan(u16, u32, lower_bounds[:, j], hg_norm[j], B=B, L=L, Lc=Lc)
            w_o = hg_w_o[j]
        elif kind == 1:
            p = dict(mu=rw_mu[j], w_rkv=rw_w_rkv[j].astype(BF16), w0=rw_w0[j], w1=rw_w1[j].astype(BF16),
                     w2=rw_w2[j].astype(BF16), a0=rw_a0[j], a1=rw_a1[j].astype(BF16), a2=rw_a2[j].astype(BF16),
                     g1=rw_g1[j].astype(BF16), g2=rw_g2[j].astype(BF16),
                     k_k=rw_k_k[j], k_a=rw_k_a[j], r_k=rw_r_k[j].reshape(D), ln_w=rw_ln_w[j], ln_b=rw_ln_b[j])
            r, k, v, g, ld, a = _rwkv_proj(X, norm1[layer], mod, p, L=L, Lc=Lc, B=B)
            y_l, y_c = _rwkv_scan(r, k, v, g, ld, a, p, B=B, L=L, Lc=Lc)
            w_o = rw_w_o[j]
        else:
            nlq = MLA_Q_LORA + MLA_KV_LORA
            wd = jnp.zeros((D, nlq + LANES), F32).at[:, :nlq].set(mla_w_dqkv[j][:, :nlq])
            wd = wd.at[:, nlq + MLA_NOPE:nlq + MLA_NOPE + MLA_ROPE].set(mla_w_dqkv[j][:, nlq:])
            wuq = mla_w_uq[j].reshape(MLA_Q_LORA, MLA_HEADS, MLA_NOPE + MLA_ROPE)
            wuq = jnp.pad(wuq, ((0, 0), (0, 0), (0, LANES - MLA_NOPE - MLA_ROPE))).reshape(MLA_Q_LORA, MLA_HEADS * LANES)
            p = dict(wd=wd.astype(BF16), q_norm=mla_q_norm[j].reshape(1, -1), kv_norm=mla_kv_norm[j].reshape(1, -1),
                     wuq=wuq.astype(BF16), wukv=mla_w_ukv[j].astype(BF16))
            tm_p = 512
            cos_t, sin_t = _rope_tables(L, tm_p)
            q, kv, kr = _mla_proj(X, norm1[layer], mod, p, cos_t, sin_t, L=L, Lc=Lc, B=B, tm=tm_p)
            y_l = _attention(q, kv, kr, B=B, L=L, Lc=Lc, latent=True)
            y_c = _attention(q, kv, kr, B=B, L=L, Lc=Lc, latent=False)
            w_o = mla_w_o[j]
        y = y_l if last else (y_l, y_c)
        X = _matmul_residual(y, w_o.astype(BF16), X, mod, gate_i=2, n_rows=n_rows, **kw)
        gmid = _ffn_gate(X, norm2[layer], mod, ffn_w_in[layer].astype(BF16), ffn_conv[layer], ffn_conv_b[layer],
                         L=L, Lc=Lc, B=B, n_rows=n_rows)
        X = _matmul_residual(gmid, ffn_w_out[layer].astype(BF16), X, mod, gate_i=5, n_rows=n_rows,
                             norm_f=norm_f if last else None, **kw)
    return X.reshape(B, L, D)
```

```python
import functools

import jax
import jax.numpy as jnp
from jax import lax
from jax.experimental import pallas as pl
from jax.experimental.pallas import tpu as pltpu

F32 = jnp.float32
BF16 = jnp.bfloat16

D_MODEL = 1024
N_MIXERS = 3
NORM_EPS = 1e-6
HG_HEADS = 8
HG_CHUNK = 64
RW_HEAD = 64
RW_LN_EPS = 64e-5
RW_CHUNK = 64
MLA_HEADS = 16
MLA_NOPE = 64
MLA_ROPE = 32
MLA_V = 64
MLA_Q_LORA = 256
MLA_KV_LORA = 256
MLA_SCALE = (MLA_NOPE + MLA_ROPE) ** -0.5
ROPE_BASE = 10000.0
LOG2E = 1.4426950408889634
GRID_W = 64
HALO = 16
LANES = 128
VMEM_LIMIT = 56 * 1024 * 1024


def _params(n_axes):
    return pltpu.CompilerParams(dimension_semantics=("arbitrary",) * n_axes, vmem_limit_bytes=VMEM_LIMIT)


def _dot(a, b):
    return jnp.dot(a, b, preferred_element_type=F32)


def _dot_nt(a, b):
    return lax.dot_general(a, b, (((1,), (1,)), ((), ())), preferred_element_type=F32)


def _dot_tn(a, b):
    return lax.dot_general(a, b, (((0,), (0,)), ((), ())), preferred_element_type=F32)


def _split(x):
    hi = x.astype(BF16)
    lo = (x - hi.astype(F32)).astype(BF16)
    return hi, lo


def _dot_exact_lhs(m_bf16, x):
    hi, lo = _split(x)
    return _dot(m_bf16, hi) + _dot(m_bf16, lo)


def _dot_exact_rhs(x, m_bf16):
    hi, lo = _split(x)
    return _dot(hi, m_bf16) + _dot(lo, m_bf16)


def _silu(x):
    return x * jax.nn.sigmoid(x)


def _norm_mod(x, g, mod_ref, shift_i, scale_i):
    ms = jnp.mean(x * x, axis=-1, keepdims=True)
    y = x * lax.rsqrt(ms + NORM_EPS) * g
    return y * (1.0 + mod_ref[0, scale_i:scale_i + 1, :]) + mod_ref[0, shift_i:shift_i + 1, :]


def _mod_index(tm, L, B):
    return lambda i, *_: (jnp.minimum(i * tm // L, B), 0, 0)


def _mod_kernel(c_ref, w_ref, b_ref, o_ref):
    s = _silu(c_ref[...]).astype(BF16)
    o_ref[0] = _dot(s, w_ref[0].astype(BF16)) + b_ref[0]


def _modulation(cc, w_mod, b_mod):
    depth, d, n = w_mod.shape
    rows = cc.shape[0]
    tn = 1024
    return pl.pallas_call(
        _mod_kernel,
        grid=(depth, n // tn),
        in_specs=[
            pl.BlockSpec((rows, d), lambda l, j: (0, 0)),
            pl.BlockSpec((1, d, tn), lambda l, j: (l, 0, j)),
            pl.BlockSpec((1, 1, tn), lambda l, j: (l, 0, j)),
        ],
        out_specs=pl.BlockSpec((1, rows, tn), lambda l, j: (l, 0, j)),
        out_shape=jax.ShapeDtypeStruct((depth, rows, n), F32),
        compiler_params=_params(2),
        name="modulation",
    )(cc, w_mod, b_mod.reshape(depth, 1, n))


def _nmm_kernel(x_ref, g_ref, mod_ref, w_ref, o16_ref, o32_ref, *, shift_i, scale_i, tn, n16):
    h = _norm_mod(x_ref[...], g_ref[...], mod_ref, shift_i, scale_i).astype(BF16)
    spt = tn // LANES
    for j in range(w_ref.shape[1] // tn):
        acc = _dot(h, w_ref[:, j * tn:(j + 1) * tn])
        o_ref, base = (o16_ref, j * spt) if j < n16 else (o32_ref, (j - n16) * spt)
        for s in range(spt):
            o_ref[base + s] = acc[:, s * LANES:(s + 1) * LANES].astype(o_ref.dtype)


def _norm_mod_matmul(x, g, mod, w, *, n_bf16, shift_i, scale_i, L, B, tm=512, tn=512):
    nt, d = x.shape
    n = w.shape[1]
    n16 = n_bf16 // tn
    assert n_bf16 % tn == 0 and n % tn == 0
    s16, s32 = n_bf16 // LANES, (n - n_bf16) // LANES
    return pl.pallas_call(
        functools.partial(_nmm_kernel, shift_i=shift_i, scale_i=scale_i, tn=tn, n16=n16),
        grid=(nt // tm,),
        in_specs=[
            pl.BlockSpec((tm, d), lambda i: (i, 0)),
            pl.BlockSpec((1, d), lambda i: (0, 0)),
            pl.BlockSpec((1, 6, d), _mod_index(tm, L, B)),
            pl.BlockSpec((d, n), lambda i: (0, 0)),
        ],
        out_specs=[
            pl.BlockSpec((s16, tm, LANES), lambda i: (0, i, 0)),
            pl.BlockSpec((s32, tm, LANES), lambda i: (0, i, 0)),
        ],
        out_shape=[
            jax.ShapeDtypeStruct((s16, nt, LANES), BF16),
            jax.ShapeDtypeStruct((s32, nt, LANES), F32),
        ],
        compiler_params=_params(1),
        name="norm_mod_matmul",
    )(x, g.reshape(1, d), mod, w)


def _mmres_kernel(*refs, gate_i, final, n_lat_tiles):
    if n_lat_tiles is None:
        a_ref, w_ref, x_ref, mod_ref, *rest = refs
        if len(a_ref.shape) == 3:
            a = jnp.concatenate([a_ref[c] for c in range(a_ref.shape[0])], axis=1)
        else:
            a = a_ref[...]
    else:
        al_ref, ac_ref, w_ref, x_ref, mod_ref, *rest = refs
        a = jnp.where(pl.program_id(0) < n_lat_tiles, al_ref[...], ac_ref[...])
    y = x_ref[...] + mod_ref[0, gate_i:gate_i + 1, :] * _dot(a, w_ref[...])
    if final:
        nf_ref, o_ref = rest
        ms = jnp.mean(y * y, axis=-1, keepdims=True)
        y = y * lax.rsqrt(ms + NORM_EPS) * nf_ref[...]
    else:
        (o_ref,) = rest
    o_ref[...] = y


def _matmul_residual(a, w, x, mod, *, gate_i, L, B, n_rows, tm=1024, norm_f=None):
    split = isinstance(a, tuple)
    blocked = not split and a.ndim == 3
    k, d = w.shape
    final = norm_f is not None
    if split:
        n_lat_tiles = a[0].shape[0] // tm
        a_specs = [
            pl.BlockSpec((tm, k), lambda i: (jnp.minimum(i, n_lat_tiles - 1), 0)),
            pl.BlockSpec((tm, k), lambda i: (jnp.maximum(i - n_lat_tiles, 0), 0)),
        ]
        a_args = list(a)
    else:
        n_lat_tiles = None
        if blocked:
            a_specs = [pl.BlockSpec((a.shape[0], tm, a.shape[2]), lambda i: (0, i, 0))]
        else:
            a_specs = [pl.BlockSpec((tm, k), lambda i: (i, 0))]
        a_args = [a]
    in_specs = a_specs + [
        pl.BlockSpec((k, d), lambda i: (0, 0)),
        pl.BlockSpec((tm, d), lambda i: (i, 0)),
        pl.BlockSpec((1, 6, d), _mod_index(tm, L, B)),
    ]
    args = a_args + [w, x, mod]
    if final:
        in_specs.append(pl.BlockSpec((1, d), lambda i: (0, 0)))
        args.append(norm_f.reshape(1, d))
    return pl.pallas_call(
        functools.partial(_mmres_kernel, gate_i=gate_i, final=final, n_lat_tiles=n_lat_tiles),
        grid=(n_rows // tm,),
        in_specs=in_specs,
        out_specs=pl.BlockSpec((tm, d), lambda i: (i, 0)),
        out_shape=jax.ShapeDtypeStruct((n_rows, d), F32),
        compiler_params=_params(1),
        name="matmul_residual",
    )(*args)


def _ffn1_kernel(x_ref, xp_ref, xn_ref, g_ref, mod_ref, w_ref, cw_ref, cb_ref, o_ref, h_scr,
                 *, tm, rb, nf, n_lat_tiles, L, Lc):
    i = pl.program_id(0)
    g = g_ref[...]
    h_scr[0:HALO, :] = _norm_mod(xp_ref[...], g, mod_ref, 3, 4).astype(BF16)
    h_scr[HALO:HALO + tm, :] = _norm_mod(x_ref[...], g, mod_ref, 3, 4).astype(BF16)
    h_scr[HALO + tm:, :] = _norm_mod(xn_ref[...], g, mod_ref, 3, 4).astype(BF16)
    is_lat = i < n_lat_tiles
    sub = lax.broadcasted_iota(jnp.int32, (8, 1), 0)
    nb = tm // rb

    def seq_edge(r0):
        return jnp.where(is_lat, ((i * tm + r0) % L) == 0, r0 % Lc == 0)

    def column_block(j, carry):
        a_blk, v_blk = [], []
        cw = cw_ref[j]
        cb = cb_ref[j]
        edge_rows = {}

        def gate(r):
            a = a_blk[r]
            up = jnp.where(seq_edge(r * rb), 0.0, a_blk[r - 1][rb - 1:rb, :] if r > 0 else edge_rows["before"])
            dn = jnp.where(seq_edge((r + 1) * rb), 0.0, a_blk[r + 1][0:1, :] if r < nb - 1 else edge_rows["after"])
            prev = pltpu.roll(a, 1, 0)
            prev = jnp.concatenate([jnp.where(sub == 0, up, prev[0:8, :]), prev[8:, :]], axis=0)
            nxt = pltpu.roll(a, rb - 1, 0)
            nxt = jnp.concatenate([nxt[:rb - 8, :], jnp.where(sub == 7, dn, nxt[rb - 8:, :])], axis=0)
            ac = cw[0:1, :] * prev + cw[1:2, :] * a + cw[2:3, :] * nxt + cb
            o_ref[j, r * rb:(r + 1) * rb, :] = (_silu(ac) * v_blk[r]).astype(BF16)

        for r in range(nb):
            lo = HALO + r * rb - (HALO if r == 0 else 0)
            hi = HALO + (r + 1) * rb + (HALO if r == nb - 1 else 0)
            a = _dot(h_scr[lo:hi, :], w_ref[j])
            if r == 0:
                edge_rows["before"], a = a[HALO - 1:HALO, :], a[HALO:, :]
            if r == nb - 1:
                edge_rows["after"], a = a[rb:rb + 1, :], a[:rb, :]
            a_blk.append(a)
            v_blk.append(_dot(h_scr[HALO + r * rb:HALO + (r + 1) * rb, :], w_ref[j + nf]))
            if r > 0:
                gate(r - 1)
        gate(nb - 1)
        return carry

    lax.fori_loop(0, nf, column_block, 0)


def _ffn_gate(x, g, mod, w_in, conv_w, conv_b, *, L, Lc, B, n_rows, tm=1024, tf=256, rb=256):
    assert L % rb == 0 and Lc % rb == 0 and tm % rb == 0
    assert tm % Lc == 0 and (L % tm == 0 or tm % L == 0)
    d = x.shape[1]
    dff = conv_w.shape[1]
    nf = dff // tf
    w_blk = w_in.reshape(d, 2 * nf, tf).transpose(1, 0, 2)
    cw_blk = conv_w.reshape(3, nf, tf).transpose(1, 0, 2)
    cb_blk = conv_b.reshape(nf, 1, tf)
    blkh = tm // HALO
    return pl.pallas_call(
        functools.partial(_ffn1_kernel, tm=tm, rb=rb, nf=nf, n_lat_tiles=B * L // tm, L=L, Lc=Lc),
        grid=(n_rows // tm,),
        in_specs=[
            pl.BlockSpec((tm, d), lambda i: (i, 0)),
            pl.BlockSpec((HALO, d), lambda i: (jnp.maximum(i * blkh - 1, 0), 0)),
            pl.BlockSpec((HALO, d), lambda i: (jnp.minimum((i + 1) * blkh, n_rows // HALO - 1), 0)),
            pl.BlockSpec((1, d), lambda i: (0, 0)),
            pl.BlockSpec((1, 6, d), _mod_index(tm, L, B)),
            pl.BlockSpec((2 * nf, d, tf), lambda i: (0, 0, 0)),
            pl.BlockSpec((nf, 3, tf), lambda i: (0, 0, 0)),
            pl.BlockSpec((nf, 1, tf), lambda i: (0, 0, 0)),
        ],
        out_specs=pl.BlockSpec((nf, tm, tf), lambda i: (0, i, 0)),
        out_shape=jax.ShapeDtypeStruct((nf, n_rows, tf), BF16),
        scratch_shapes=[pltpu.VMEM((tm + 2 * HALO, d), BF16)],
        compiler_params=_params(1),
        name="ffn_gate",
    )(x, x, x, g.reshape(1, d), mod, w_blk, cw_blk, cb_blk)


def _tri(n, rev):
    t = lax.broadcasted_iota(jnp.int32, (n, n), 0)
    s = lax.broadcasted_iota(jnp.int32, (n, n), 1)
    return (s >= t) if rev else (s <= t)


def _gla_group(q, k, v, lf, rev):
    n = len(q)
    c = q[0].shape[0]
    mid = c // 2
    tri = {r: _tri(c, r) for r in set(rev)}
    tri_b = {r: m.astype(BF16) for r, m in tri.items()}
    mid_row = [(c - 1 - mid) if rev[i] else mid for i in range(n)]
    end_row = [0 if rev[i] else c - 1 for i in range(n)]
    b = [_dot_exact_lhs(tri_b[rev[i]], lf[i]) for i in range(n)]
    b_mid = [b[i][mid_row[i]:mid_row[i] + 1, :] for i in range(n)]
    b_end = [b[i][end_row[i]:end_row[i] + 1, :] for i in range(n)]
    qs = [(q[i] * jnp.exp(b[i] - b_mid[i])).astype(BF16) for i in range(n)]
    ks = [(k[i] * jnp.exp(b_mid[i] - b[i])).astype(BF16) for i in range(n)]
    att = [_dot_nt(qs[i], ks[i]) for i in range(n)]
    att = [jnp.where(tri[rev[i]], att[i], 0.0).astype(BF16) for i in range(n)]
    vb = [v[i].astype(BF16) for i in range(n)]
    o_intra = [_dot(att[i], vb[i]) for i in range(n)]
    kd = [(k[i] * jnp.exp(b_end[i] - b[i])).astype(BF16) for i in range(n)]
    ds = [_dot_tn(vb[i], kd[i]) for i in range(n)]
    qd = [(q[i] * jnp.exp(b[i])).astype(BF16) for i in range(n)]
    e_end = [jnp.exp(b_end[i]) for i in range(n)]
    return o_intra, qd, ds, e_end


def _hgrn_kernel(ql, il, gl, zfl, zbl, qc, ic, gc, zfc, zbc, lb_ref, gn_ref, ol_ref, oc_ref,
                 of_l, ob_l, of_c, ob_c, sf, sb):
    C = HG_CHUNK
    lbf = lb_ref[0, 0]
    lbb = lb_ref[1, 0]

    def gates(z, lb):
        return jnp.log(lb + (1.0 - lb) * jax.nn.sigmoid(z)), (1.0 - lb) * jax.nn.sigmoid(-z)

    def part(q_ref, i_ref, zf_ref, zb_ref, of_ref, ob_ref, U):
        n = q_ref.shape[0] // C

        def body(t, carry):
            rows = [pl.ds(pl.multiple_of((t * U + u) * C, C), C) for u in range(U)]
            rows += [pl.ds(pl.multiple_of((n - 1 - t * U - u) * C, C), C) for u in range(U)]
            rev = [False] * U + [True] * U
            zs = [(zb_ref if r else zf_ref)[rw, :] for rw, r in zip(rows, rev)]
            gk = [gates(z, lbb if r else lbf) for z, r in zip(zs, rev)]
            q = [_silu(q_ref[rw, :].astype(F32)) for rw in rows]
            v = [i_ref[rw, :] for rw in rows]
            o_intra, qd, ds, e_end = _gla_group(q, [g[1] for g in gk], v, [g[0] for g in gk], rev)
            st = [sf[...], sb[...]]
            for u in range(U):
                for d in range(2):
                    i = d * U + u
                    o = o_intra[i] + _dot_nt(qd[i], st[d].astype(BF16))
                    (ob_ref if d else of_ref)[rows[i], :] = o
                    st[d] = st[d] * e_end[i] + ds[i]
            sf[...] = st[0]
            sb[...] = st[1]
            return carry

        lax.fori_loop(0, n // U, body, 0)

    sf[...] = jnp.zeros_like(sf)
    sb[...] = jnp.zeros_like(sb)
    part(qc, ic, zfc, zbc, of_c, ob_c, 4)
    part(ql, il, zfl, zbl, of_l, ob_l, 8)

    def readout(of_ref, ob_ref, g_ref, o_ref):
        o = of_ref[...] + ob_ref[...]
        ms = jnp.mean(o * o, axis=-1, keepdims=True)
        y = o * lax.rsqrt(ms + NORM_EPS) * gn_ref[...]
        o_ref[...] = (y * _silu(g_ref[...].astype(F32))).astype(BF16)

    readout(of_c, ob_c, gc, oc_ref)
    readout(of_l, ob_l, gl, ol_ref)


def _hgrn_scan(u16, u32, lbs, g_norm, *, B, L, Lc):
    H = HG_HEADS
    ctx0 = B * L // Lc

    def lat(sec):
        return pl.BlockSpec((None, L, LANES), lambda b, h: (sec * H + h, b, 0))

    def ctx(sec):
        return pl.BlockSpec((None, Lc, LANES), lambda b, h: (sec * H + h, ctx0 + b, 0))

    secs = (0, 1, 2, 0, 1)
    arrays = [u16, u16, u16, u32, u32]
    return pl.pallas_call(
        _hgrn_kernel,
        grid=(B, H),
        in_specs=[lat(s) for s in secs] + [ctx(s) for s in secs] + [
            pl.BlockSpec((2, 1, 1, LANES), lambda b, h: (0, h, 0, 0)),
            pl.BlockSpec((1, LANES), lambda b, h: (0, 0)),
        ],
        out_specs=[
            pl.BlockSpec((L, LANES), lambda b, h: (b, h)),
            pl.BlockSpec((Lc, LANES), lambda b, h: (b, h)),
        ],
        out_shape=[
            jax.ShapeDtypeStruct((B * L, H * LANES), BF16),
            jax.ShapeDtypeStruct((B * Lc, H * LANES), BF16),
        ],
        scratch_shapes=[
            pltpu.VMEM((L, LANES), F32), pltpu.VMEM((L, LANES), F32),
            pltpu.VMEM((Lc, LANES), F32), pltpu.VMEM((Lc, LANES), F32),
            pltpu.VMEM((LANES, LANES), F32), pltpu.VMEM((LANES, LANES), F32),
        ],
        compiler_params=_params(2),
        name="hgrn_scan",
    )(*(arrays * 2), lbs.reshape(2, H, 1, LANES), g_norm.reshape(1, LANES))


def _softplus(x):
    return jnp.maximum(x, 0.0) + jnp.log1p(jnp.exp(-jnp.abs(x)))


def _rwkv_proj_kernel(x_ref, xp_ref, xn_ref, g_ref, mod_ref, mu_ref, wrkv_ref, w0_ref, w1_ref, w2_ref,
                      a0_ref, a1_ref, a2_ref, g1_ref, g2_ref,
                      r_ref, k_ref, v_ref, gg_ref, ld_ref, a_ref, *, tm, n_lat_tiles, L, Lc):
    i = pl.program_id(0)
    g = g_ref[...]
    h = _norm_mod(x_ref[...], g, mod_ref, 0, 1)
    hp = _norm_mod(xp_ref[...], g, mod_ref, 0, 1)[7:8, :]
    hn = _norm_mod(xn_ref[...], g, mod_ref, 0, 1)[0:1, :]
    seqlen = jnp.where(i < n_lat_tiles, L, Lc)
    row = lax.broadcasted_iota(jnp.int32, (tm, 1), 0)
    pos = (i * tm + row) & (seqlen - 1)
    down = jnp.where(row == 0, hp, pltpu.roll(h, 1, 0))
    down = jnp.where(pos == 0, 0.0, down)
    up = jnp.where(row == tm - 1, hn, pltpu.roll(h, tm - 1, 0))
    up = jnp.where(pos == seqlen - 1, 0.0, up)
    dx = 0.5 * (down + up) - h

    def mix(j):
        return (h + dx * mu_ref[j:j + 1, :]).astype(BF16)

    xw, xa, xg = mix(1), mix(4), mix(5)
    t_w = [_dot(xw, w1_ref[d]) for d in range(2)]
    t_a = [_dot(xa, a1_ref[d]) for d in range(2)]
    t_g = _dot(xg, g1_ref[...])
    wl = [_dot(jnp.tanh(t_w[d]).astype(BF16), w2_ref[d]) for d in range(2)]
    al = [_dot(t_a[d].astype(BF16), a2_ref[d]) for d in range(2)]
    gg_ref[...] = _dot(jax.nn.sigmoid(t_g).astype(BF16), g2_ref[...])
    r = _dot(mix(0), wrkv_ref[0])
    k = _dot(mix(2), wrkv_ref[1])
    v = _dot(mix(3), wrkv_ref[2])
    for d in range(2):
        w_log = -_softplus(-(w0_ref[d:d + 1, :] + wl[d])) - 0.5
        ld_ref[d] = -jnp.exp(w_log)
        a_ref[d] = jax.nn.sigmoid(a0_ref[d:d + 1, :] + al[d])
    r_ref[...] = r
    k_ref[...] = k
    v_ref[...] = v


def _rwkv_proj(x, g, mod, p, *, L, Lc, B, tm=256):
    nt, d = x.shape
    blk8 = tm // 8

    def full(a):
        nd = a.ndim
        return pl.BlockSpec(a.shape, lambda i: (0,) * nd)

    weights = [p["mu"], p["w_rkv"], p["w0"], p["w1"], p["w2"], p["a0"], p["a1"], p["a2"], p["g1"], p["g2"]]
    row_spec = pl.BlockSpec((tm, d), lambda i: (i, 0))
    dir_spec = pl.BlockSpec((2, tm, d), lambda i: (0, i, 0))
    return pl.pallas_call(
        functools.partial(_rwkv_proj_kernel, tm=tm, n_lat_tiles=B * L // tm, L=L, Lc=Lc),
        grid=(nt // tm,),
        in_specs=[
            row_spec,
            pl.BlockSpec((8, d), lambda i: (jnp.maximum(i * blk8 - 1, 0), 0)),
            pl.BlockSpec((8, d), lambda i: (jnp.minimum((i + 1) * blk8, nt // 8 - 1), 0)),
            pl.BlockSpec((1, d), lambda i: (0, 0)),
            pl.BlockSpec((1, 6, d), _mod_index(tm, L, B)),
        ] + [full(w) for w in weights],
        out_specs=[row_spec, row_spec, row_spec, row_spec, dir_spec, dir_spec],
        out_shape=[jax.ShapeDtypeStruct((nt, d), F32)] * 4 + [jax.ShapeDtypeStruct((2, nt, d), F32)] * 2,
        compiler_params=_params(1),
        name="rwkv_proj",
    )(x, x, x, g.reshape(1, d), mod, *weights)


def _rwkv_scan_kernel(rl, kl, vl, gl, ldl, al, rc, kc, vc, gc, ldc, ac,
                      kk_ref, ka_ref, rk_ref, lnw_ref, lnb_ref, ol_ref, oc_ref,
                      G, Hs, Q, Y1, Yo, z_scr):
    C = RW_CHUNK
    C2 = 2 * C
    Lc = rc.shape[0]
    L = rl.shape[0]
    nc, nl = Lc // C, L // C
    lane = lax.broadcasted_iota(jnp.int32, (1, LANES), 1)
    m0 = lane < RW_HEAD
    ri = lax.broadcasted_iota(jnp.int32, (LANES, LANES), 0)
    ci = lax.broadcasted_iota(jnp.int32, (LANES, LANES), 1)
    same_head = (ri < RW_HEAD) == (ci < RW_HEAD)
    eye = ri == ci
    ones_bd = same_head.astype(BF16)
    k_k = kk_ref[...]
    k_a = ka_ref[...]

    def segsum(x):
        return _dot_exact_rhs(x, ones_bd)

    def x2(x):
        return jnp.concatenate([jnp.where(m0, x, 0.0), jnp.where(m0, 0.0, x)], axis=0)

    def t2(x):
        return jnp.concatenate([x, x], axis=0)

    def sel(x):
        return jnp.where(m0, x[:C, :], x[C:, :])

    tt = lax.broadcasted_iota(jnp.int32, (C2, C2), 0)
    ss = lax.broadcasted_iota(jnp.int32, (C2, C2), 1)
    sh = (tt < C) == (ss < C)
    tl, sl = tt & (C - 1), ss & (C - 1)
    strict = {False: sh & (sl < tl), True: sh & (sl > tl)}
    incl = {False: sh & (sl <= tl), True: sh & (sl >= tl)}
    eye2 = jnp.where(tt == ss, 1.0, 0.0)
    tri_b = {r: _tri(C, r).astype(BF16) for r in (False, True)}

    def phase1(insts):
        n = range(len(insts))
        rev = [d == 1 for (_, d, _, _) in insts]
        rows = [pl.ds(r0, C) for (_, _, r0, _) in insts]
        r = [insts[i][0][0][rows[i], :] for i in n]
        k = [insts[i][0][1][rows[i], :] for i in n]
        v = [insts[i][0][2][rows[i], :] for i in n]
        lw = [insts[i][0][3][insts[i][1], rows[i], :] for i in n]
        a = [insts[i][0][4][insts[i][1], rows[i], :] for i in n]
        kk = [k[i] * k_k for i in n]
        n2 = [segsum(kk[i] * kk[i]) for i in n]
        yield
        kk = [kk[i] / jnp.maximum(jnp.sqrt(n2[i]), 1e-12) for i in n]
        kmod = [k[i] * (1.0 + (a[i] - 1.0) * k_a) for i in n]
        beta = [kk[i] * a[i] for i in n]
        c_in = [_dot_exact_lhs(tri_b[rev[i]], lw[i]) for i in n]
        yield
        c_end = [c_in[i][(0 if rev[i] else C - 1):(0 if rev[i] else C - 1) + 1, :] for i in n]
        e_neg = [jnp.exp(-c_in[i]) for i in n]
        e_end = [jnp.exp(c_end[i] - c_in[i]) for i in n]
        ea = [-kk[i] * jnp.exp(c_in[i] - lw[i]) for i in n]
        rb = [r[i] * jnp.exp(c_in[i]) for i in n]
        lhs = [jnp.concatenate([x2(ea[i]), x2(rb[i])], axis=0).astype(BF16) for i in n]
        rhs = [jnp.concatenate([t2(beta[i] * e_neg[i]), t2(kmod[i] * e_neg[i])], axis=0).astype(BF16) for i in n]
        M = [_dot_nt(lhs[i], rhs[i]) for i in n]
        yield
        A = [jnp.where(strict[rev[i]], M[i][:C2, :C2], 0.0) for i in n]
        Bm = [jnp.where(strict[rev[i]], M[i][:C2, C2:], 0.0).astype(BF16) for i in n]
        arbr = [jnp.concatenate([jnp.where(incl[rev[i]], M[i][C2:, :C2], 0.0),
                                 jnp.where(incl[rev[i]], M[i][C2:, C2:], 0.0)], axis=1).astype(BF16) for i in n]
        T = [eye2 + A[i] for i in n]
        P = A
        for _ in range(C.bit_length() - 2):
            Pb = [P[i].astype(BF16) for i in n]
            P = [_dot(Pb[i], Pb[i]) for i in n]
            yield
            T = [T[i] + _dot(T[i].astype(BF16), P[i].astype(BF16)) for i in n]
            yield
        vb2 = [t2(v[i]).astype(BF16) for i in n]
        X1 = [_dot(Bm[i], vb2[i]) for i in n]
        yield
        TU = [_dot(T[i].astype(BF16), jnp.concatenate([X1[i], t2(ea[i])], axis=1).astype(BF16)) for i in n]
        yield
        YQ = [_dot(arbr[i], jnp.concatenate(
            [TU[i].astype(BF16), jnp.concatenate([vb2[i], jnp.zeros_like(vb2[i])], axis=1)], axis=0)) for i in n]
        bk = [jnp.concatenate([beta[i] * e_end[i], kmod[i] * e_end[i]], axis=0).astype(BF16) for i in n]
        right = [jnp.concatenate([
            jnp.concatenate([sel(TU[i][:, LANES:]), sel(TU[i][:, :LANES])], axis=1),
            jnp.concatenate([jnp.zeros_like(v[i]), v[i]], axis=1)], axis=0).astype(BF16) for i in n]
        GH = [_dot_tn(bk[i], right[i]) for i in n]
        yield
        for i in n:
            _, d, _, pos = insts[i]
            qrows = pl.ds(pl.multiple_of(pos * C, C), C)
            Y1[d, qrows, :] = sel(YQ[i][:, :LANES])
            Q[d, qrows, :] = rb[i] + sel(YQ[i][:, LANES:])
            G[d, pos] = jnp.where(eye, jnp.exp(c_end[i]), 0.0) + jnp.where(same_head, GH[i][:, :LANES], 0.0)
            Hs[d, pos] = jnp.where(same_head, GH[i][:, LANES:], 0.0)

    def phase2(positions):
        z = [z_scr[0], z_scr[1]]
        for pos in positions:
            for d in range(2):
                qrows = pl.ds(pl.multiple_of(pos * C, C), C)
                if d == 0:
                    dest = pos * C
                else:
                    dest = jnp.where(pos < nc, (nc - 1 - pos) * C, Lc + (nc + nl - 1 - pos) * C)
                Yo[d, pl.ds(pl.multiple_of(dest, C), C), :] = _dot(Q[d, qrows, :].astype(BF16), z[d].astype(BF16)) + Y1[d, qrows, :]
                g_hi, g_lo = _split(G[d, pos])
                z_hi, z_lo = _split(z[d])
                z[d] = _dot(g_hi, z_hi) + _dot(g_hi, z_lo) + _dot(g_lo, z_hi) + Hs[d, pos]
            yield
        z_scr[0] = z[0]
        z_scr[1] = z[1]

    def run(items, positions):
        g2 = phase2(positions) if positions else iter(())
        if items:
            for stage, _ in enumerate(phase1(items)):
                if stage % 3 == 1:
                    next(g2, None)
        for _ in g2:
            pass

    lat_refs = (rl, kl, vl, ldl, al)
    ctx_refs = (rc, kc, vc, ldc, ac)
    U = 4
    assert nc == U and nl % U == 0

    z_scr[...] = jnp.zeros_like(z_scr)
    items = []
    for u in range(U):
        items += [(ctx_refs, 0, u * C, u), (ctx_refs, 1, (nc - 1 - u) * C, u)]
    run(items, None)

    def body(t, carry):
        items = []
        for u in range(U):
            items += [(lat_refs, 0, pl.multiple_of((t * U + u) * C, C), nc + t * U + u),
                      (lat_refs, 1, pl.multiple_of((nl - 1 - t * U - u) * C, C), nc + t * U + u)]
        run(items, [t * U + u for u in range(U)])
        return carry

    lax.fori_loop(0, nl // U, body, 0)
    run(None, [nl + u for u in range(U)])

    inv_n = 1.0 / RW_HEAD
    RB = 256

    def readout(refs, g_ref, y0, o_ref):
        r_ref, k_ref, v_ref, _, a_ref = refs

        def body(t, carry):
            rows = pl.ds(pl.multiple_of(t * RB, RB), RB)
            yrows = pl.ds(pl.multiple_of(y0 + t * RB, RB), RB)
            y = Yo[0, yrows, :] + Yo[1, yrows, :]
            mean = segsum(y) * inv_n
            yc = y - mean
            var = segsum(yc * yc) * inv_n
            yn = yc * lax.rsqrt(var + RW_LN_EPS) * lnw_ref[...] + lnb_ref[...]
            k = k_ref[rows, :]
            k_sum = k * (1.0 + (a_ref[0, rows, :] - 1.0) * k_a) + k * (1.0 + (a_ref[1, rows, :] - 1.0) * k_a)
            bonus = segsum(r_ref[rows, :] * k_sum * rk_ref[...]) * v_ref[rows, :]
            o_ref[rows, :] = ((yn + bonus) * g_ref[rows, :]).astype(BF16)
            return carry

        lax.fori_loop(0, r_ref.shape[0] // RB, body, 0)

    readout(ctx_refs, gc, 0, oc_ref)
    readout(lat_refs, gl, Lc, ol_ref)


def _rwkv_scan(r, k, v, g, ld, a, p, *, B, L, Lc):
    npair = D_MODEL // LANES
    ctx0 = B * L // Lc
    n_chunks = (L + Lc) // RW_CHUNK
    lat = pl.BlockSpec((L, LANES), lambda b, h: (b, h))
    ctx = pl.BlockSpec((Lc, LANES), lambda b, h: (ctx0 + b, h))
    lat2 = pl.BlockSpec((2, L, LANES), lambda b, h: (0, b, h))
    ctx2 = pl.BlockSpec((2, Lc, LANES), lambda b, h: (0, ctx0 + b, h))
    vec = pl.BlockSpec((1, LANES), lambda b, h: (0, h))
    vecs = [p["k_k"], p["k_a"], p["r_k"], p["ln_w"], p["ln_b"]]
    return pl.pallas_call(
        _rwkv_scan_kernel,
        grid=(B, npair),
        in_specs=[lat, lat, lat, lat, lat2, lat2, ctx, ctx, ctx, ctx, ctx2, ctx2] + [vec] * 5,
        out_specs=[
            pl.BlockSpec((L, LANES), lambda b, h: (b, h)),
            pl.BlockSpec((Lc, LANES), lambda b, h: (b, h)),
        ],
        out_shape=[
            jax.ShapeDtypeStruct((B * L, D_MODEL), BF16),
            jax.ShapeDtypeStruct((B * Lc, D_MODEL), BF16),
        ],
        scratch_shapes=[
            pltpu.VMEM((2, n_chunks, LANES, LANES), F32),
            pltpu.VMEM((2, n_chunks, LANES, LANES), F32),
            pltpu.VMEM((2, L + Lc, LANES), F32),
            pltpu.VMEM((2, L + Lc, LANES), F32),
            pltpu.VMEM((2, L + Lc, LANES), F32),
            pltpu.VMEM((2, LANES, LANES), F32),
        ],
        compiler_params=_params(2),
        name="rwkv_scan",
    )(r, k, v, g, ld, a, r, k, v, g, ld, a, *[x.reshape(1, D_MODEL) for x in vecs])


def _rope_rot(x, lane):
    n = x.shape[-1]
    w = (lane - MLA_NOPE) & 15
    return jnp.where(w < 8, -pltpu.roll(x, n - 8, 1), pltpu.roll(x, 8, 1))


def _mla_proj_kernel(x_ref, g_ref, mod_ref, wd_ref, qn_ref, kvn_ref, wuq_ref, wukv_ref, cos_ref, sin_ref,
                     q_ref, kv_ref, kr_ref):
    h = _norm_mod(x_ref[...], g_ref[...], mod_ref, 0, 1).astype(BF16)
    dq = _dot(h, wd_ref[...])
    cq = dq[:, :MLA_Q_LORA]
    ckv = dq[:, MLA_Q_LORA:MLA_Q_LORA + MLA_KV_LORA]
    kr = dq[:, MLA_Q_LORA + MLA_KV_LORA:]

    def rms(x, w):
        return x * lax.rsqrt(jnp.mean(x * x, axis=-1, keepdims=True) + NORM_EPS) * w

    q = _dot(rms(cq, qn_ref[...]).astype(BF16), wuq_ref[...])
    kv_ref[...] = _dot(rms(ckv, kvn_ref[...]).astype(BF16), wukv_ref[...]).astype(BF16)
    cos = cos_ref[...]
    sin = sin_ref[...]
    lane = lax.broadcasted_iota(jnp.int32, (1, LANES), 1)
    kr_ref[...] = (kr * cos + _rope_rot(kr, lane) * sin).astype(BF16)
    lane_q = lax.broadcasted_iota(jnp.int32, (1, q.shape[1]), 1) & (LANES - 1)
    reps = q.shape[1] // LANES
    cos_q = jnp.concatenate([cos] * reps, axis=1)
    sin_q = jnp.concatenate([sin] * reps, axis=1)
    q_ref[...] = ((q * cos_q + _rope_rot(q, lane_q) * sin_q) * (MLA_SCALE * LOG2E)).astype(BF16)


def _mla_proj(x, g, mod, p, cos_t, sin_t, *, L, Lc, B, tm=512):
    nt, d = x.shape
    n_lat_tiles = B * L // tm
    per_seq = L // tm
    nq = MLA_HEADS * LANES

    def full(a):
        nd = a.ndim
        return pl.BlockSpec(a.shape, lambda i: (0,) * nd)

    def tab(i):
        return (jnp.where(i < n_lat_tiles, i % per_seq, per_seq), 0)

    weights = [p["wd"], p["q_norm"], p["kv_norm"], p["wuq"], p["wukv"]]
    return pl.pallas_call(
        _mla_proj_kernel,
        grid=(nt // tm,),
        in_specs=[
            pl.BlockSpec((tm, d), lambda i: (i, 0)),
            pl.BlockSpec((1, d), lambda i: (0, 0)),
            pl.BlockSpec((1, 6, d), _mod_index(tm, L, B)),
        ] + [full(w) for w in weights] + [pl.BlockSpec((tm, LANES), tab), pl.BlockSpec((tm, LANES), tab)],
        out_specs=[
            pl.BlockSpec((tm, nq), lambda i: (i, 0)),
            pl.BlockSpec((tm, nq), lambda i: (i, 0)),
            pl.BlockSpec((tm, LANES), lambda i: (i, 0)),
        ],
        out_shape=[
            jax.ShapeDtypeStruct((nt, nq), BF16),
            jax.ShapeDtypeStruct((nt, nq), BF16),
            jax.ShapeDtypeStruct((nt, LANES), BF16),
        ],
        compiler_params=_params(1),
        name="mla_proj",
    )(x, g.reshape(1, d), mod, *weights, cos_t, sin_t)


def _attn_kernel(*refs, has_lat):
    if has_lat:
        q_ref, kvc_ref, krc_ref, kvl_ref, krl_ref, o_ref = refs[:6]
    else:
        q_ref, kvc_ref, krc_ref, o_ref = refs[:4]
    lane = lax.broadcasted_iota(jnp.int32, (1, LANES), 1)
    m0 = lane < MLA_NOPE
    heads = range(2)
    cols = [slice(hh * LANES, (hh + 1) * LANES) for hh in heads]
    q = [q_ref[:, cols[hh]] for hh in heads]
    kvs = [[kvc_ref[:, cols[hh]] for hh in heads]]
    krs = [krc_ref[...]]
    if has_lat:
        kvs.append([kvl_ref[:, cols[hh]] for hh in heads])
        krs.append(krl_ref[...])
    kf_refs = refs[-len(kvs):]

    @pl.when(pl.program_id(2) == 0)
    def _():
        for kf_ref, kv, kr in zip(kf_refs, kvs, krs):
            for hh in heads:
                kf_ref[hh] = jnp.where(m0, kv[hh], kr)

    st = [[_dot_nt(kf_ref[hh], q[hh]) for kf_ref in kf_refs] for hh in heads]
    outs = []
    for hh in heads:
        m = st[hh][0].max(axis=0, keepdims=True)
        for s_ in st[hh][1:]:
            m = jnp.maximum(m, s_.max(axis=0, keepdims=True))
        den = None
        acc = None
        for part, s_ in enumerate(st[hh]):
            p = jnp.exp2(s_ - m)
            ps = p.sum(axis=0, keepdims=True)
            den = ps if den is None else den + ps
            o = _dot_tn(kvs[part][hh], p.astype(BF16))
            acc = o if acc is None else acc + o
        outs.append((acc / den)[MLA_NOPE:, :])
    o_ref[...] = jnp.concatenate(outs, axis=0).T.astype(BF16)


def _attention(q, kv, kr, *, B, L, Lc, latent, tq=1024):
    npair = MLA_HEADS // 2
    ctx0 = B * L // Lc
    w2 = 2 * LANES
    kv_ctx = pl.BlockSpec((Lc, w2), lambda b, h, t: (ctx0 + b, h))
    kr_ctx = pl.BlockSpec((Lc, LANES), lambda b, h, t: (ctx0 + b, 0))
    if latent:
        nq_t = L // tq
        in_specs = [
            pl.BlockSpec((tq, w2), lambda b, h, t: (b * nq_t + t, h)),
            kv_ctx, kr_ctx,
            pl.BlockSpec((L, w2), lambda b, h, t: (b, h)),
            pl.BlockSpec((L, LANES), lambda b, h, t: (b, 0)),
        ]
        args = (q, kv, kr, kv, kr)
        rows = B * L
        out_spec = pl.BlockSpec((tq, LANES), lambda b, h, t: (b * nq_t + t, h))
    else:
        nq_t = 1
        in_specs = [pl.BlockSpec((Lc, w2), lambda b, h, t: (ctx0 + b, h)), kv_ctx, kr_ctx]
        args = (q, kv, kr)
        rows = B * Lc
        out_spec = pl.BlockSpec((Lc, LANES), lambda b, h, t: (b, h))
    return pl.pallas_call(
        functools.partial(_attn_kernel, has_lat=latent),
        grid=(B, npair, nq_t),
        in_specs=in_specs,
        out_specs=out_spec,
        out_shape=jax.ShapeDtypeStruct((rows, npair * LANES), BF16),
        scratch_shapes=[pltpu.VMEM((2, Lc, LANES), BF16)] + ([pltpu.VMEM((2, L, LANES), BF16)] if latent else []),
        compiler_params=_params(3),
        name="mla_attention_lat" if latent else "mla_attention_ctx",
    )(*args)


def _rope_tables(L, tm):
    n_rows = L // GRID_W
    row = jnp.repeat(jnp.arange(n_rows, dtype=F32), GRID_W)
    col = jnp.tile(jnp.arange(GRID_W, dtype=F32), n_rows)
    nq = MLA_ROPE // 4
    inv_freq = ROPE_BASE ** (-jnp.arange(nq, dtype=F32) / nq)
    ang_r = row[:, None] * inv_freq
    ang_c = col[:, None] * inv_freq
    ang = jnp.concatenate([ang_r, ang_r, ang_c, ang_c], axis=-1)
    cos = jnp.ones((L + tm, LANES), F32).at[:L, MLA_NOPE:MLA_NOPE + MLA_ROPE].set(jnp.cos(ang))
    sin = jnp.zeros((L + tm, LANES), F32).at[:L, MLA_NOPE:MLA_NOPE + MLA_ROPE].set(jnp.sin(ang))
    return cos, sin


def kernel(x, c, ctx, c_ctx, w_mod, b_mod, norm1, norm2, ffn_w_in, ffn_conv, ffn_conv_b, ffn_w_out, hg_w_in, hg_lb, hg_norm, hg_w_o, rw_mu, rw_w_rkv, rw_w0, rw_w1, rw_w2, rw_a0, rw_a1, rw_a2, rw_g1, rw_g2, rw_k_k, rw_k_a, rw_r_k, rw_ln_w, rw_ln_b, rw_w_o, mla_w_dqkv, mla_q_norm, mla_kv_norm, mla_w_uq, mla_w_ukv, mla_w_o, norm_f):
    B, L, D = x.shape
    Lc = ctx.shape[1]
    depth = w_mod.shape[0]
    assert D == D_MODEL and L & (L - 1) == 0 and Lc & (Lc - 1) == 0 and L % 2048 == 0 and (B * Lc) % 2048 == 0
    n_lat = B * L
    nt = n_lat + B * Lc

    X = jnp.concatenate([x.reshape(n_lat, D), ctx.reshape(B * Lc, D)], axis=0)

    mod_rows = 8 * ((B + 1 + 7) // 8)
    cc = jnp.zeros((mod_rows, D), F32).at[:B].set(c).at[B].set(c_ctx)
    mod_all = _modulation(cc, w_mod, b_mod)[:, :B + 1].reshape(depth, B + 1, 6, D)

    lb_p = jnp.cumsum(jax.nn.softmax(hg_lb.astype(F32), axis=1), axis=1)
    lower_bounds = lb_p - lb_p[:, :1]

    for layer in range(depth):
        last = layer == depth - 1
        kind, j = layer % N_MIXERS, layer // N_MIXERS
        mod = mod_all[layer]
        n_rows = n_lat if last else nt
        kw = dict(L=L, B=B)
        if kind == 0:
            u16, u32 = _norm_mod_matmul(X, norm1[layer], mod, hg_w_in[j].astype(BF16), n_bf16=3 * D, shift_i=0, scale_i=1, **kw)
            y_l, y_c = _hgrn_scan(u16, u32, lower_bounds[:, j], hg_norm[j], B=B, L=L, Lc=Lc)
            w_o = hg_w_o[j]
        elif kind == 1:
            p = dict(mu=rw_mu[j], w_rkv=rw_w_rkv[j].astype(BF16), w0=rw_w0[j], w1=rw_w1[j].astype(BF16),
                     w2=rw_w2[j].astype(BF16), a0=rw_a0[j], a1=rw_a1[j].astype(BF16), a2=rw_a2[j].astype(BF16),
                     g1=rw_g1[j].astype(BF16), g2=rw_g2[j].astype(BF16),
                     k_k=rw_k_k[j], k_a=rw_k_a[j], r_k=rw_r_k[j].reshape(D), ln_w=rw_ln_w[j], ln_b=rw_ln_b[j])
            r, k, v, g, ld, a = _rwkv_proj(X, norm1[layer], mod, p, L=L, Lc=Lc, B=B)
            y_l, y_c = _rwkv_scan(r, k, v, g, ld, a, p, B=B, L=L, Lc=Lc)
            w_o = rw_w_o[j]
        else:
            nlq = MLA_Q_LORA + MLA_KV_LORA
            wd = jnp.zeros((D, nlq + LANES), F32).at[:, :nlq].set(mla_w_dqkv[j][:, :nlq])
            wd = wd.at[:, nlq + MLA_NOPE:nlq + MLA_NOPE + MLA_ROPE].set(mla_w_dqkv[j][:, nlq:])
            wuq = mla_w_uq[j].reshape(MLA_Q_LORA, MLA_HEADS, MLA_NOPE + MLA_ROPE)
            wuq = jnp.pad(wuq, ((0, 0), (0, 0), (0, LANES - MLA_NOPE - MLA_ROPE))).reshape(MLA_Q_LORA, MLA_HEADS * LANES)
            p = dict(wd=wd.astype(BF16), q_norm=mla_q_norm[j].reshape(1, -1), kv_norm=mla_kv_norm[j].reshape(1, -1),
                     wuq=wuq.astype(BF16), wukv=mla_w_ukv[j].astype(BF16))
            tm_p = 512
            cos_t, sin_t = _rope_tables(L, tm_p)
            q, kv, kr = _mla_proj(X, norm1[layer], mod, p, cos_t, sin_t, L=L, Lc=Lc, B=B, tm=tm_p)
            y_l = _attention(q, kv, kr, B=B, L=L, Lc=Lc, latent=True)
            y_c = _attention(q, kv, kr, B=B, L=L, Lc=Lc, latent=False)
            w_o = mla_w_o[j]
        y = y_l if last else (y_l, y_c)
        X = _matmul_residual(y, w_o.astype(BF16), X, mod, gate_i=2, n_rows=n_rows, **kw)
        gmid = _ffn_gate(X, norm2[layer], mod, ffn_w_in[layer].astype(BF16), ffn_conv[layer], ffn_conv_b[layer],
                         L=L, Lc=Lc, B=B, n_rows=n_rows)
        X = _matmul_residual(gmid, ffn_w_out[layer].astype(BF16), X, mod, gate_i=5, n_rows=n_rows,
                             norm_f=norm_f if last else None, **kw)
    return X.reshape(B, L, D)
```
